```python
import jax, jax.numpy as jnp
from jax import lax
import numpy as np

D_MODEL = 1024
BATCH = 8
SEQ = 2048
DEPTH = 4
DEC_BATCH = 32
DEC_SEQ = 4
PAST_LEN = 8192
PAGE_SIZE = 128

RET_HEADS = 4
RET_QK_DIM = 128
RET_V_DIM = 256
RET_CHUNK = 128
SGU_GROUPS = 4
SGU_WIDTH = D_MODEL
SGU_CHUNK = 128
DIL_HEADS = 8
DIL_HEAD_DIM = 128
DIL_PATTERNS = ((128, 1), (512, 4), (2048, 16))
DIL_MAX_WINDOW = 2048
DIL_BLOCK = 128
ROPE_THETA = 10000.0
N_BRANCHES = 3
BRANCH_WIDTH = D_MODEL
N_EXPERTS = 32
TOP_K = 4
D_FF = D_MODEL
SWIGLU_ALPHA = 1.702
SWIGLU_LIMIT = 7.0
MOE_BLOCK = 128
PLE_DIM = 256
DEEPNORM_ALPHA = (2 * DEPTH) ** 0.25
DEEPNORM_BETA = (8 * DEPTH) ** -0.25
LN_EPS = 1e-5

IN_COLS = (RET_HEADS * RET_QK_DIM, RET_HEADS * RET_QK_DIM, RET_HEADS * RET_V_DIM, RET_HEADS * RET_V_DIM,
           SGU_WIDTH, SGU_WIDTH,
           DIL_HEADS * DIL_HEAD_DIM, DIL_HEADS * DIL_HEAD_DIM, DIL_HEADS * DIL_HEAD_DIM,
           N_BRANCHES * D_MODEL)
N_IN_COLS = sum(IN_COLS)
IN_SPLIT_POINTS = tuple(int(c) for c in np.cumsum(IN_COLS)[:-1])

kernel_name = 'retention_sgu_dilated_moe_deepnorm_step'


def layer_norm(x, g, b):
    xf = x.astype(jnp.float32)
    mu = jnp.mean(xf, axis=-1, keepdims=True)
    var = jnp.mean(jnp.square(xf - mu), axis=-1, keepdims=True)
    return ((xf - mu) * lax.rsqrt(var + LN_EPS) * g + b).astype(x.dtype)


def group_norm(x):
    mu = jnp.mean(x, axis=-1, keepdims=True)
    var = jnp.mean(jnp.square(x - mu), axis=-1, keepdims=True)
    return (x - mu) * lax.rsqrt(var + LN_EPS)


def rotate(x, pos, inv_freq):
    ang = pos.astype(jnp.float32)[:, None] * inv_freq[None, :]
    cos, sin = jnp.cos(ang)[:, None, :], jnp.sin(ang)[:, None, :]
    x1, x2 = jnp.split(x, 2, axis=-1)
    return jnp.concatenate([x1 * cos - x2 * sin, x2 * cos + x1 * sin], axis=-1)


def retention_inv_freq():
    return 1.0 / (10000.0 ** jnp.linspace(0.0, 1.0, RET_QK_DIM // 2, dtype=jnp.float32))


def rope_inv_freq(hd):
    return ROPE_THETA ** (-jnp.arange(0, hd, 2, dtype=jnp.float32) / hd)


def retention_log_decay():
    return jnp.log1p(-jnp.exp2(-5.0 - jnp.arange(RET_HEADS, dtype=jnp.float32)))


def retention_chunk(state, q, k, v, log_g):
    C = q.shape[1]
    n = jnp.arange(C, dtype=jnp.float32)
    diff = n[:, None] - n[None, :]
    decay = jnp.where(diff >= 0, jnp.exp(log_g[:, None, None] * jnp.maximum(diff, 0.0)), 0.0)
    scores = jnp.einsum('bqhd,bkhd->bhqk', q, k) * decay
    o = jnp.einsum('bhqk,bkhv->bqhv', scores, v)
    o = o + jnp.einsum('bqhd,bhdv->bqhv', q, state) * jnp.exp(log_g[None, :] * (n[:, None] + 1.0))[None, :, :, None]
    k_w = k * jnp.exp(log_g[None, :] * (C - 1.0 - n[:, None]))[None, :, :, None]
    new_state = state * jnp.exp(log_g * C)[None, :, None, None] + jnp.einsum('bkhd,bkhv->bhdv', k_w, v)
    return o, new_state


def retention_prompt(q, k, v, log_g):
    B, S = q.shape[:2]
    nc = S // RET_CHUNK

    def to_chunks(t):
        return jnp.moveaxis(t.reshape(B, nc, RET_CHUNK, *t.shape[2:]), 1, 0)

    state0 = jnp.zeros((B, RET_HEADS, RET_QK_DIM, RET_V_DIM), jnp.float32)

    def step(st, qkv):
        o, st = retention_chunk(st, qkv[0], qkv[1], qkv[2], log_g)
        return st, o

    st, o = lax.scan(step, state0, (to_chunks(q), to_chunks(k), to_chunks(v)))
    return jnp.moveaxis(o, 0, 1).reshape(B, S, RET_HEADS, RET_V_DIM), st


def sgu_spatial(v_chunks, w_s, b_s):
    B, nc, C, W = v_chunks.shape
    w = jnp.where(jnp.tril(jnp.ones((SGU_CHUNK, SGU_CHUNK), bool)), w_s, 0.0)[:, :C, :C]
    vg = v_chunks.reshape(B, nc, C, SGU_GROUPS, W // SGU_GROUPS)
    f = jnp.einsum('gts,bnsgc->bntgc', w, vg) + jnp.swapaxes(b_s[:, :C], 0, 1)[:, :, None]
    return f.reshape(B, nc * C, W)


def dilated_branch_prompt(q, k, v, window, dil):
    B, S, H, hd = q.shape
    L = S // dil
    nblk = -(-L // DIL_BLOCK)
    Lp = nblk * DIL_BLOCK

    def to_blocks(t):
        t = t.reshape(B, L, dil, H, hd).transpose(0, 2, 1, 3, 4)
        t = jnp.pad(t, ((0, 0), (0, 0), (0, Lp - L), (0, 0), (0, 0)))
        return t.reshape(B, dil, nblk, DIL_BLOCK, H, hd)

    def with_prev(t):
        prev = jnp.pad(t, ((0, 0), (0, 0), (1, 0), (0, 0), (0, 0), (0, 0)))[:, :, :-1]
        return jnp.concatenate([prev, t], axis=3)

    qb = to_blocks(q)
    kk, vv = with_prev(to_blocks(k)), with_prev(to_blocks(v))
    s = jnp.einsum('brnqhe,brnkhe->brnhqk', qb, kk) * hd ** -0.5
    qi = jnp.arange(DIL_BLOCK)[:, None]
    kj = jnp.arange(2 * DIL_BLOCK)[None, :]
    dist = qi + DIL_BLOCK - kj
    kpos = jnp.arange(nblk)[:, None, None] * DIL_BLOCK + kj[None] - DIL_BLOCK
    mask = (dist >= 0) & (dist <= window // dil) & (kpos >= 0)
    s = jnp.where(mask[None, None, :, None], s, -jnp.inf)
    m = jnp.max(s, axis=-1, keepdims=True)
    e = jnp.exp(s - m)
    den = jnp.sum(e, axis=-1)
    o = jnp.einsum('brnhqk,brnkhe->brnqhe', e, vv) / jnp.moveaxis(den, -1, -2)[..., None]
    lse = jnp.moveaxis(m[..., 0] + jnp.log(den), -1, -2)

    def to_positions(t):
        t = t.reshape(B, dil, Lp, *t.shape[4:])[:, :, :L]
        return jnp.swapaxes(t, 1, 2).reshape(B, S, *t.shape[3:])

    return to_positions(o), to_positions(lse)


def dilated_branch_sample(q, kc, vc, window, dil, w_buf):
    T, hd = q.shape[1], q.shape[-1]
    n_keys = window // dil + 1
    idx = w_buf + jnp.arange(T)[:, None] - dil * jnp.arange(n_keys)[None, :]
    valid = idx >= 0
    idx = jnp.maximum(idx, 0)
    kg, vg = kc[:, idx], vc[:, idx]
    s = jnp.einsum('bthe,btkhe->bthk', q, kg) * hd ** -0.5
    s = jnp.where(valid[None, :, None, :], s, -jnp.inf)
    m = jnp.max(s, axis=-1, keepdims=True)
    e = jnp.exp(s - m)
    den = jnp.sum(e, axis=-1)
    o = jnp.einsum('bthk,btkhe->bthe', e, vg) / den[..., None]
    return o, m[..., 0] + jnp.log(den)


def combine_by_denominator(outs, lses):
    w = jax.nn.softmax(jnp.stack(lses), axis=0)
    return jnp.einsum('pbsh,pbshe->bshe', w, jnp.stack(outs))


def token_mixer(x, pos, lw, ret_state, dil_past_k, dil_past_v, sample):
    B, S, _ = x.shape
    f32 = jnp.float32
    a_q, a_k, a_v, a_g, b_u, b_v, c_q, c_k, c_v, gate = jnp.split(
        jnp.einsum('bsd,dc->bsc', x, lw['w_in']), IN_SPLIT_POINTS, axis=-1)
    rq = rotate(a_q.reshape(B, S, RET_HEADS, RET_QK_DIM).astype(f32), pos, retention_inv_freq())
    rk = rotate(a_k.reshape(B, S, RET_HEADS, RET_QK_DIM).astype(f32), pos, retention_inv_freq()) * RET_QK_DIM ** -0.5
    rv = a_v.reshape(B, S, RET_HEADS, RET_V_DIM).astype(f32)
    log_g = retention_log_decay()
    if sample:
        ret_o, ret_new = retention_chunk(ret_state.astype(f32), rq, rk, rv, log_g)
    else:
        ret_o, ret_new = retention_prompt(rq, rk, rv, log_g)
    out_a = (group_norm(ret_o).reshape(B, S, RET_HEADS * RET_V_DIM) * jax.nn.silu(a_g.astype(f32))).astype(x.dtype)
    u = jax.nn.gelu(b_u)
    v_n = layer_norm(jax.nn.gelu(b_v), lw['sgu_ln_g'], lw['sgu_ln_b'])
    chunk = S if sample else SGU_CHUNK
    f = sgu_spatial(v_n.reshape(B, S // chunk, chunk, SGU_WIDTH), lw['sgu_w'], lw['sgu_b'])
    out_b = (u * f).astype(x.dtype)
    inv = rope_inv_freq(DIL_HEAD_DIM)
    cq = rotate(c_q.reshape(B, S, DIL_HEADS, DIL_HEAD_DIM).astype(f32), pos, inv)
    ck = rotate(c_k.reshape(B, S, DIL_HEADS, DIL_HEAD_DIM).astype(f32), pos, inv)
    cv = c_v.reshape(B, S, DIL_HEADS, DIL_HEAD_DIM).astype(f32)
    if sample:
        w_buf = dil_past_k.shape[1]
        kc = jnp.concatenate([dil_past_k.astype(f32), ck], axis=1)
        vc = jnp.concatenate([dil_past_v.astype(f32), cv], axis=1)
        res = [dilated_branch_sample(cq, kc, vc, w, d, w_buf) for (w, d) in DIL_PATTERNS]
        k_rows, v_rows = ck, cv
    else:
        res = [dilated_branch_prompt(cq, ck, cv, w, d) for (w, d) in DIL_PATTERNS]
        keep = min(DIL_MAX_WINDOW, S)
        k_rows, v_rows = ck[:, S - keep:], cv[:, S - keep:]
    out_c = combine_by_denominator([r[0] for r in res], [r[1] for r in res])
    out_c = out_c.reshape(B, S, DIL_HEADS * DIL_HEAD_DIM).astype(x.dtype)
    branches = jnp.stack([out_a, out_b, out_c], axis=2)
    proj = jnp.einsum('bsnw,nwd->bsnd', branches, lw['w_branch'])
    gates = jax.nn.sigmoid(gate.reshape(B, S, N_BRANCHES, D_MODEL))
    merged = jnp.sum(gates * proj, axis=2)
    y = jnp.einsum('bsd,de->bse', merged, lw['w_out'])
    return y, ret_new, k_rows.astype(x.dtype), v_rows.astype(x.dtype), v_n


def moe(h, router_w, router_b, w_up, b_up, w_down, b_down):
    B, S, D = h.shape
    T = B * S
    x = h.reshape(T, D)
    logits = (x @ router_w + router_b).astype(jnp.float32)
    top_logit, top_idx = lax.top_k(logits, TOP_K)
    gate = jax.nn.softmax(top_logit, axis=-1)
    n_pairs = T * TOP_K
    e_flat = top_idx.reshape(n_pairs)
    tok_flat = jnp.arange(n_pairs, dtype=jnp.int32) // TOP_K
    order = jnp.argsort(e_flat)
    e_sorted = e_flat[order]
    counts = jax.ops.segment_sum(jnp.ones((n_pairs,), jnp.int32), e_flat, num_segments=N_EXPERTS)
    padded = (counts + MOE_BLOCK - 1) // MOE_BLOCK * MOE_BLOCK
    pad_end = jnp.cumsum(padded)
    pad_start = pad_end - padded
    start = jnp.cumsum(counts) - counts
    dest = pad_start[e_sorted] + jnp.arange(n_pairs, dtype=jnp.int32) - start[e_sorted]
    n_blocks = -(-n_pairs // MOE_BLOCK) + N_EXPERTS
    n_rows = n_blocks * MOE_BLOCK
    row_tok = jnp.full((n_rows,), T, jnp.int32).at[dest].set(tok_flat[order])
    row_w = jnp.zeros((n_rows,), jnp.float32).at[dest].set(gate.reshape(n_pairs)[order])
    block_expert = jnp.minimum(
        jnp.searchsorted(pad_end, jnp.arange(n_blocks, dtype=jnp.int32) * MOE_BLOCK, side='right'), N_EXPERTS - 1)
    x_rows = jnp.concatenate([x, jnp.zeros((1, D), x.dtype)], axis=0)[row_tok].reshape(n_blocks, MOE_BLOCK, D)

    def expert_block(args):
        xb, e = args
        up = xb @ w_up[e] + b_up[e]
        glu, lin = jnp.split(up, 2, axis=-1)
        glu = jnp.minimum(glu, SWIGLU_LIMIT)
        lin = jnp.clip(lin, -SWIGLU_LIMIT, SWIGLU_LIMIT)
        act = glu * jax.nn.sigmoid(SWIGLU_ALPHA * glu) * (lin + 1.0)
        return act @ w_down[e] + b_down[e]

    y_rows = lax.map(expert_block, (x_rows, block_expert)).reshape(n_rows, D)
    y = jax.ops.segment_sum(y_rows * row_w[:, None].astype(y_rows.dtype), row_tok, num_segments=T + 1)[:T]
    return y.reshape(B, S, D).astype(h.dtype)


def decoder_layer(x, p_i, pos, lw, ret_state, dil_past_k, dil_past_v, sample):
    mix, ret_new, k_rows, v_rows, sgu_v = token_mixer(x, pos, lw, ret_state, dil_past_k, dil_past_v, sample)
    h = layer_norm(DEEPNORM_ALPHA * x + mix, lw['ln1_g'], lw['ln1_b'])
    ffn = moe(h, lw['router_w'], lw['router_b'], lw['exp_w_up'], lw['exp_b_up'], lw['exp_w_down'], lw['exp_b_down'])
    ple = jax.nn.sigmoid(jnp.einsum('bsd,de->bse', h, lw['ple_w_gate'])) * jnp.einsum('bsp,pd->bsd', p_i, lw['ple_w_proj'])
    out = layer_norm(DEEPNORM_ALPHA * h + ffn + ple, lw['ln2_g'], lw['ln2_b'])
    return out, ret_new, k_rows, v_rows, sgu_v


def setup_inputs(seed: int = 0) -> dict:
    key = jax.random.key(seed)
    ks = iter(jax.random.split(key, 32))

    def nrm(shape, scale):
        return jax.random.normal(next(ks), shape, jnp.float32) * scale

    w_buf = min(DIL_MAX_WINDOW, PAST_LEN)
    return {
        'x_prompt': nrm((BATCH, SEQ, D_MODEL), 1.0),
        'x_sample': nrm((DEC_BATCH, DEC_SEQ, D_MODEL), 1.0),
        'state_ret': nrm((DEPTH, DEC_BATCH, RET_HEADS, RET_QK_DIM, RET_V_DIM), 0.5),
        'cache_dil_k': nrm((DEPTH, DEC_BATCH, w_buf, DIL_HEADS, DIL_HEAD_DIM), 1.0),
        'cache_dil_v': nrm((DEPTH, DEC_BATCH, w_buf, DIL_HEADS, DIL_HEAD_DIM), 1.0),
        'p_prompt': nrm((DEPTH, BATCH, SEQ, PLE_DIM), 1.0),
        'p_sample': nrm((DEPTH, DEC_BATCH, DEC_SEQ, PLE_DIM), 1.0),
        'w_in': nrm((DEPTH, D_MODEL, N_IN_COLS), D_MODEL ** -0.5),
        'sgu_ln_g': 1.0 + nrm((DEPTH, SGU_WIDTH), 0.01),
        'sgu_ln_b': nrm((DEPTH, SGU_WIDTH), 0.01),
        'sgu_w': nrm((DEPTH, SGU_GROUPS, SGU_CHUNK, SGU_CHUNK), SGU_CHUNK ** -0.5),
        'sgu_b': 1.0 + nrm((DEPTH, SGU_GROUPS, SGU_CHUNK), 0.01),
        'w_branch': nrm((DEPTH, N_BRANCHES, BRANCH_WIDTH, D_MODEL), BRANCH_WIDTH ** -0.5 * DEEPNORM_BETA),
        'w_out': nrm((DEPTH, D_MODEL, D_MODEL), D_MODEL ** -0.5 * DEEPNORM_BETA),
        'ln1_g': 1.0 + nrm((DEPTH, D_MODEL), 0.01),
        'ln1_b': nrm((DEPTH, D_MODEL), 0.01),
        'router_w': nrm((DEPTH, D_MODEL, N_EXPERTS), D_MODEL ** -0.5),
        'router_b': nrm((DEPTH, N_EXPERTS), 0.01),
        'exp_w_up': nrm((DEPTH, N_EXPERTS, D_MODEL, 2 * D_FF), D_MODEL ** -0.5),
        'exp_b_up': nrm((DEPTH, N_EXPERTS, 2 * D_FF), 0.01),
        'exp_w_down': nrm((DEPTH, N_EXPERTS, D_FF, D_MODEL), D_FF ** -0.5 * DEEPNORM_BETA),
        'exp_b_down': nrm((DEPTH, N_EXPERTS, D_MODEL), 0.01),
        'ple_w_gate': nrm((DEPTH, D_MODEL, D_MODEL), D_MODEL ** -0.5),
        'ple_w_proj': nrm((DEPTH, PLE_DIM, D_MODEL), PLE_DIM ** -0.5 * DEEPNORM_BETA),
        'ln2_g': 1.0 + nrm((DEPTH, D_MODEL), 0.01),
        'ln2_b': nrm((DEPTH, D_MODEL), 0.01),
    }


def reference(x_prompt, x_sample, state_ret, cache_dil_k, cache_dil_v, p_prompt, p_sample,
              w_in, sgu_ln_g, sgu_ln_b, sgu_w, sgu_b, w_branch, w_out, ln1_g, ln1_b,
              router_w, router_b, exp_w_up, exp_b_up, exp_w_down, exp_b_down,
              ple_w_gate, ple_w_proj, ln2_g, ln2_b):
    pos_prompt = jnp.arange(x_prompt.shape[1], dtype=jnp.int32)
    pos_sample = PAST_LEN + jnp.arange(x_sample.shape[1], dtype=jnp.int32)
    y_p, y_s = x_prompt, x_sample
    ret_p, ret_s, k_p, v_p, k_s, v_s, sgu_s = [], [], [], [], [], [], []
    for i in range(DEPTH):
        lw = dict(w_in=w_in[i], sgu_ln_g=sgu_ln_g[i], sgu_ln_b=sgu_ln_b[i], sgu_w=sgu_w[i], sgu_b=sgu_b[i],
                  w_branch=w_branch[i], w_out=w_out[i], ln1_g=ln1_g[i], ln1_b=ln1_b[i],
                  router_w=router_w[i], router_b=router_b[i], exp_w_up=exp_w_up[i], exp_b_up=exp_b_up[i],
                  exp_w_down=exp_w_down[i], exp_b_down=exp_b_down[i], ple_w_gate=ple_w_gate[i],
                  ple_w_proj=ple_w_proj[i], ln2_g=ln2_g[i], ln2_b=ln2_b[i])
        y_p, st, kr, vr, _ = decoder_layer(y_p, p_prompt[i], pos_prompt, lw, None, None, None, False)
        ret_p.append(st.astype(x_prompt.dtype))
        k_p.append(kr)
        v_p.append(vr)
        y_s, st, kr, vr, sv = decoder_layer(y_s, p_sample[i], pos_sample, lw, state_ret[i], cache_dil_k[i], cache_dil_v[i], True)
        ret_s.append(st.astype(state_ret.dtype))
        k_s.append(kr)
        v_s.append(vr)
        sgu_s.append(sv)
    return (y_p, y_s, jnp.stack(ret_p), jnp.stack(ret_s), jnp.stack(k_p), jnp.stack(v_p),
            jnp.stack(k_s), jnp.stack(v_s), jnp.stack(sgu_s))
```

```python
import functools

import numpy as np
import jax
import jax.numpy as jnp
from jax import lax
from jax.experimental import pallas as pl
from jax.experimental.pallas import tpu as pltpu

F32, BF16 = jnp.float32, jnp.bfloat16

D_MODEL = 1024
PAST_LEN = 8192
RET_HEADS, RET_QK, RET_V, RET_CHUNK = 4, 128, 256, 128
SGU_GROUPS, SGU_CHUNK = 4, 128
SGU_GW = D_MODEL // SGU_GROUPS
DIL_HEADS, DIL_HD, DIL_BLOCK = 8, 128, 128
DIL_PATTERNS = ((128, 1), (512, 4), (2048, 16))
ROPE_THETA = 10000.0
N_BRANCHES = 3
N_EXPERTS, TOP_K, D_FF = 32, 4, 1024
SWIGLU_ALPHA, SWIGLU_LIMIT = 1.702, 7.0
PLE_DIM = 256
LN_EPS = 1e-5

C_AQ, C_AK, C_AV, C_AG = 0, 512, 1024, 2048
C_BU, C_BV = 3072, 4096
C_CQ, C_CK, C_CV = 5120, 6144, 7168
C_GATE = 8192
N_IN_COLS = C_GATE + N_BRANCHES * D_MODEL

LANES = 128
MOE_ROWS_PROMPT = 256
MOE_ROWS_SAMPLE = 128
GATHER_ROWS = 256


def _cparams(sem, vmem_mb=None):
    kw = dict(dimension_semantics=sem)
    if vmem_mb is not None:
        kw["vmem_limit_bytes"] = vmem_mb * 2 ** 20
    return pltpu.CompilerParams(**kw)


def _dot(a, b):
    return jnp.dot(a, b, preferred_element_type=F32)


def _dot_nt(a, b):
    return lax.dot_general(a, b, (((1,), (1,)), ((), ())), preferred_element_type=F32)


def _dot_tn(a, b):
    return lax.dot_general(a, b, (((0,), (0,)), ((), ())), preferred_element_type=F32)


def _ln(x, g, b):
    mu = jnp.mean(x, axis=-1, keepdims=True)
    xc = x - mu
    var = jnp.mean(xc * xc, axis=-1, keepdims=True)
    return xc * lax.rsqrt(var + LN_EPS) * g + b


def _rot(x, cosf, sinf):
    return x * cosf + pltpu.roll(x, LANES // 2, 1) * sinf


def _rot_tables(pos, inv_freq):
    ang = pos.astype(F32)[:, None] * inv_freq[None, :]
    cos, sin = jnp.cos(ang), jnp.sin(ang)
    return jnp.concatenate([cos, cos], axis=-1), jnp.concatenate([-sin, sin], axis=-1)


def _retention_inv_freq():
    return 1.0 / (10000.0 ** jnp.linspace(0.0, 1.0, RET_QK // 2, dtype=F32))


def _rope_inv_freq(hd):
    return ROPE_THETA ** (-jnp.arange(0, hd, 2, dtype=F32) / hd)


def _retention_tables(C):
    log_g = jnp.log1p(-jnp.exp2(-5.0 - jnp.arange(RET_HEADS, dtype=F32)))
    n = jnp.arange(C, dtype=F32)
    diff = n[:, None] - n[None, :]
    decay = jnp.where(diff >= 0, jnp.exp(log_g[:, None, None] * jnp.maximum(diff, 0.0)), 0.0)
    qscale = jnp.exp(log_g[:, None] * (n[None, :] + 1.0))[:, :, None]
    kscale = jnp.exp(log_g[:, None] * (C - 1.0 - n[None, :]))[:, :, None]
    gc = jnp.exp(log_g * C)[:, None, None]
    return decay, qscale, kscale, gc


def _mm_kernel(x_ref, w_ref, o_ref):
    o_ref[...] = _dot(x_ref[...], w_ref[...]).astype(o_ref.dtype)


def _in_proj(x_bf, w_in_bf, layer):
    T = x_bf.shape[0]
    tm = min(T, 512)
    tn = 1024
    return pl.pallas_call(
        _mm_kernel,
        grid=(N_IN_COLS // tn, T // tm),
        in_specs=[pl.BlockSpec((tm, D_MODEL), lambda n, t: (t, 0)),
                  pl.BlockSpec((None, D_MODEL, tn), lambda n, t: (layer, 0, n))],
        out_specs=pl.BlockSpec((tm, tn), lambda n, t: (t, n)),
        out_shape=jax.ShapeDtypeStruct((T, N_IN_COLS), BF16),
        compiler_params=_cparams(("parallel", "arbitrary")),
        name="in_proj",
    )(x_bf, w_in_bf)


def _group_norm_gate(o, g):
    mu = jnp.mean(o, axis=-1, keepdims=True)
    oc = o - mu
    var = jnp.mean(oc * oc, axis=-1, keepdims=True)
    return oc * lax.rsqrt(var + LN_EPS) * jax.nn.silu(g)


def _ret_prompt_kernel(q_ref, k_ref, v_ref, g_ref, cos_ref, sin_ref, decay_ref, qs_ref, ks_ref, gc_ref,
                       o_ref, st_ref, state, *, n_chunks):
    c = pl.program_id(1)

    @pl.when(c == 0)
    def _():
        state[...] = jnp.zeros_like(state)

    cosf, sinf = cos_ref[...], sin_ref[...]
    for h in range(RET_HEADS):
        qk = slice(h * RET_QK, (h + 1) * RET_QK)
        vv = slice(h * RET_V, (h + 1) * RET_V)
        q = _rot(q_ref[:, qk].astype(F32), cosf, sinf)
        k = _rot(k_ref[:, qk].astype(F32), cosf, sinf) * RET_QK ** -0.5
        v = v_ref[:, vv]
        qb = q.astype(BF16)
        s = _dot_nt(qb, k.astype(BF16)) * decay_ref[h]
        o = _dot(s.astype(BF16), v)
        st = state[h]
        o = o + _dot(qb, st.astype(BF16)) * qs_ref[h]
        kw = (k * ks_ref[h]).astype(BF16)
        state[h] = st * gc_ref[h] + _dot_tn(kw, v)
        o_ref[:, vv] = _group_norm_gate(o, g_ref[:, vv].astype(F32)).astype(o_ref.dtype)

    @pl.when(c == n_chunks - 1)
    def _():
        st_ref[...] = state[...]


def _retention_prompt(cols, B, S, cosf, sinf, tabs):
    C = RET_CHUNK
    nc = S // C
    decay, qscale, kscale, gc = tabs
    H = RET_HEADS
    row = lambda b, c: b * nc + c
    return pl.pallas_call(
        functools.partial(_ret_prompt_kernel, n_chunks=nc),
        grid=(B, nc),
        in_specs=[
            pl.BlockSpec((C, H * RET_QK), lambda b, c: (row(b, c), C_AQ // (H * RET_QK))),
            pl.BlockSpec((C, H * RET_QK), lambda b, c: (row(b, c), C_AK // (H * RET_QK))),
            pl.BlockSpec((C, H * RET_V), lambda b, c: (row(b, c), C_AV // (H * RET_V))),
            pl.BlockSpec((C, H * RET_V), lambda b, c: (row(b, c), C_AG // (H * RET_V))),
            pl.BlockSpec((C, RET_QK), lambda b, c: (c, 0)),
            pl.BlockSpec((C, RET_QK), lambda b, c: (c, 0)),
            pl.BlockSpec((H, C, C), lambda b, c: (0, 0, 0)),
            pl.BlockSpec((H, C, 1), lambda b, c: (0, 0, 0)),
            pl.BlockSpec((H, C, 1), lambda b, c: (0, 0, 0)),
            pl.BlockSpec((H, 1, 1), lambda b, c: (0, 0, 0)),
        ],
        out_specs=[
            pl.BlockSpec((C, H * RET_V), lambda b, c: (row(b, c), 0)),
            pl.BlockSpec((None, H, RET_QK, RET_V), lambda b, c: (b, 0, 0, 0)),
        ],
        out_shape=[jax.ShapeDtypeStruct((B * S, H * RET_V), BF16),
                   jax.ShapeDtypeStruct((B, H, RET_QK, RET_V), F32)],
        scratch_shapes=[pltpu.VMEM((H, RET_QK, RET_V), F32)],
        compiler_params=_cparams(("parallel", "arbitrary")),
        name="retention_prompt",
    )(cols, cols, cols, cols, cosf, sinf, decay, qscale, kscale, gc)


def _sgu_weights(w_ref):
    r = lax.broadcasted_iota(jnp.int32, (SGU_CHUNK, SGU_CHUNK), 0)
    c = lax.broadcasted_iota(jnp.int32, (SGU_CHUNK, SGU_CHUNK), 1)
    return [jnp.where(r >= c, w_ref[g], 0.0) for g in range(SGU_GROUPS)]


def _sgu_prompt_kernel(u_ref, v_ref, lg_ref, lb_ref, w_ref, bs_ref, o_ref, *, n_chunks):
    w = [x.astype(BF16) for x in _sgu_weights(w_ref)]
    for c in range(n_chunks):
        rows = slice(c * SGU_CHUNK, (c + 1) * SGU_CHUNK)
        u = jax.nn.gelu(u_ref[rows, :].astype(F32))
        vn = _ln(jax.nn.gelu(v_ref[rows, :].astype(F32)), lg_ref[...], lb_ref[...])
        for g in range(SGU_GROUPS):
            cs = slice(g * SGU_GW, (g + 1) * SGU_GW)
            f = _dot(w[g], vn[:, cs].astype(BF16)) + bs_ref[g]
            o_ref[rows, cs] = (u[:, cs] * f).astype(o_ref.dtype)


def _sgu_prompt(cols, T, sgu_ln_g, sgu_ln_b, sgu_w, sgu_b4, layer):
    n_chunks = 4
    R = n_chunks * SGU_CHUNK
    return pl.pallas_call(
        functools.partial(_sgu_prompt_kernel, n_chunks=n_chunks),
        grid=(T // R,),
        in_specs=[
            pl.BlockSpec((R, D_MODEL), lambda t: (t, C_BU // D_MODEL)),
            pl.BlockSpec((R, D_MODEL), lambda t: (t, C_BV // D_MODEL)),
            pl.BlockSpec((None, 1, D_MODEL), lambda t: (layer, 0, 0)),
            pl.BlockSpec((None, 1, D_MODEL), lambda t: (layer, 0, 0)),
            pl.BlockSpec((None, SGU_GROUPS, SGU_CHUNK, SGU_CHUNK), lambda t: (layer, 0, 0, 0)),
            pl.BlockSpec((None, SGU_GROUPS, SGU_CHUNK, 1), lambda t: (layer, 0, 0, 0)),
        ],
        out_specs=pl.BlockSpec((R, D_MODEL), lambda t: (t, 0)),
        out_shape=jax.ShapeDtypeStruct((T, D_MODEL), BF16),
        compiler_params=_cparams(("parallel",)),
        name="sgu_prompt",
    )(cols, cols, sgu_ln_g, sgu_ln_b, sgu_w, sgu_b4)


def _dil_prompt_kernel(q_ref, k_ref, v_ref, cos_ref, sin_ref, o_ref, ko_ref, vo_ref,
                       qp, kp, vp, qs, ks, vs, m_s, l_s, a_s, m_t, l_t, a_t, m_r, l_r, a_r, *, S):
    BLK = DIL_BLOCK
    nblk = S // BLK
    scale = DIL_HD ** -0.5
    RC = 256

    def rot_body(j, carry):
        r = pl.ds(pl.multiple_of(j * RC, RC), RC)
        cosf, sinf = cos_ref[r, :], sin_ref[r, :]
        q = _rot(q_ref[r, :].astype(F32), cosf, sinf)
        k = _rot(k_ref[r, :].astype(F32), cosf, sinf)
        v = v_ref[r, :].astype(F32)
        qp[r, :] = q
        kp[r, :] = k
        vp[r, :] = v
        qs[r, :] = q.astype(BF16)
        ks[r, :] = k.astype(BF16)
        vs[r, :] = v.astype(BF16)
        ko_ref[r, :] = k
        vo_ref[r, :] = v
        return carry

    lax.fori_loop(0, S // RC, rot_body, 0)

    qi = lax.broadcasted_iota(jnp.int32, (BLK, BLK), 0)
    kj = lax.broadcasted_iota(jnp.int32, (BLK, BLK), 1)
    lower = kj <= qi
    upper = kj >= qi

    def attend(nps, m_d, l_d, a_d):
        def blk(j, carry):
            r0 = pl.multiple_of(j * BLK, BLK)
            cur = pl.ds(r0, BLK)
            qb = qs[cur, :]
            vc = vs[cur, :]
            s_c = jnp.where(lower, _dot_nt(qb, ks[cur, :]) * scale, -jnp.inf)
            m = jnp.max(s_c, axis=-1, keepdims=True)
            if nps > 1:
                prev = pl.ds(pl.multiple_of(jnp.maximum(r0 - BLK, 0), BLK), BLK)
                no_prev = jnp.where((j % nps) != 0, 0.0, -jnp.inf)
                s_p = jnp.where(upper, _dot_nt(qb, ks[prev, :]) * scale + no_prev, -jnp.inf)
                m = jnp.maximum(m, jnp.max(s_p, axis=-1, keepdims=True))
            e_c = jnp.exp(s_c - m)
            l = jnp.sum(e_c, axis=-1, keepdims=True)
            acc = _dot(e_c.astype(BF16), vc)
            if nps > 1:
                e_p = jnp.exp(s_p - m)
                l = l + jnp.sum(e_p, axis=-1, keepdims=True)
                acc = acc + _dot(e_p.astype(BF16), vs[prev, :])
            m_d[cur, :] = m
            l_d[cur, :] = l
            a_d[cur, :] = acc
            return carry

        lax.fori_loop(0, nblk, blk, 0)

    def deinterleave(d):
        L = S // d
        for r in range(d):
            dst = slice(r * L, (r + 1) * L)
            src = pl.ds(r, L, stride=d)
            qs[dst, :] = qp[src, :].astype(BF16)
            ks[dst, :] = kp[src, :].astype(BF16)
            vs[dst, :] = vp[src, :].astype(BF16)

    def interleave(d):
        L = S // d
        for r in range(d):
            src = slice(r * L, (r + 1) * L)
            dst = pl.ds(r, L, stride=d)
            m_t[dst, :] = m_s[src, :]
            l_t[dst, :] = l_s[src, :]
            a_t[dst, :] = a_s[src, :]

    def merge(final):
        m_x, l_x, a_x = m_t, l_t, a_t

        def body(j, carry):
            r = pl.ds(pl.multiple_of(j * BLK, BLK), BLK)
            m_old, m_new = m_r[r, :], m_x[r, :]
            m = jnp.maximum(m_old, m_new)
            a, b = jnp.exp(m_old - m), jnp.exp(m_new - m)
            l = a * l_r[r, :] + b * l_x[r, :]
            acc = a * a_r[r, :] + b * a_x[r, :]
            if final:
                o_ref[r, :] = (acc / l).astype(o_ref.dtype)
            else:
                m_r[r, :] = m
                l_r[r, :] = l
                a_r[r, :] = acc
            return carry

        lax.fori_loop(0, nblk, body, 0)

    attend(S // BLK, m_r, l_r, a_r)
    for k, (_, d) in enumerate(DIL_PATTERNS[1:]):
        deinterleave(d)
        attend(S // d // BLK, m_s, l_s, a_s)
        interleave(d)
        merge(final=(k == len(DIL_PATTERNS) - 2))


def _dilated_prompt(cols, B, S, cosf, sinf):
    H, hd = DIL_HEADS, DIL_HD
    assert all(w // d == DIL_BLOCK for w, d in DIL_PATTERNS) and S % (16 * DIL_BLOCK) == 0
    col = lambda off: (lambda b, h: (b, off // hd + h))
    vm = lambda shape, dt: pltpu.VMEM(shape, dt)
    return pl.pallas_call(
        functools.partial(_dil_prompt_kernel, S=S),
        grid=(B, H),
        in_specs=[
            pl.BlockSpec((S, hd), col(C_CQ)),
            pl.BlockSpec((S, hd), col(C_CK)),
            pl.BlockSpec((S, hd), col(C_CV)),
            pl.BlockSpec((S, hd), lambda b, h: (0, 0)),
            pl.BlockSpec((S, hd), lambda b, h: (0, 0)),
        ],
        out_specs=[pl.BlockSpec((S, hd), lambda b, h: (b, h))] * 3,
        out_shape=[jax.ShapeDtypeStruct((B * S, H * hd), BF16),
                   jax.ShapeDtypeStruct((B * S, H * hd), F32),
                   jax.ShapeDtypeStruct((B * S, H * hd), F32)],
        scratch_shapes=[vm((S, hd), F32)] * 3 + [vm((S, hd), BF16)] * 3
        + [vm((S, 1), F32), vm((S, 1), F32), vm((S, hd), F32)] * 3,
        compiler_params=_cparams(("parallel", "arbitrary"), vmem_mb=48),
        name="dilated_prompt",
    )(cols, cols, cols, cosf, sinf)


def _ret_sgu_sample_kernel(cols_ref, st_ref, cos_ref, sin_ref, decay_ref, qs_ref, ks_ref, gc_ref,
                           lg_ref, lb_ref, w_ref, bs_ref, oa_ref, ob_ref, sto_ref, vn_ref, *, C):
    cosf, sinf = cos_ref[...], sin_ref[...]
    for h in range(RET_HEADS):
        q = _rot(cols_ref[:, C_AQ + h * RET_QK:C_AQ + (h + 1) * RET_QK].astype(F32), cosf, sinf)
        k = _rot(cols_ref[:, C_AK + h * RET_QK:C_AK + (h + 1) * RET_QK].astype(F32), cosf, sinf) * RET_QK ** -0.5
        v = cols_ref[:, C_AV + h * RET_V:C_AV + (h + 1) * RET_V].astype(F32)
        g = cols_ref[:, C_AG + h * RET_V:C_AG + (h + 1) * RET_V].astype(F32)
        s = _dot_nt(q.astype(BF16), k.astype(BF16)) * decay_ref[h]
        o = s[:, 0:1] * v[0:1, :]
        for j in range(1, C):
            o = o + s[:, j:j + 1] * v[j:j + 1, :]
        st = st_ref[h]
        o = o + _dot(q.astype(BF16), st.astype(BF16)) * qs_ref[h]
        kw = (k * ks_ref[h]).astype(BF16)
        sto_ref[h] = st * gc_ref[h] + _dot_tn(kw, v.astype(BF16))
        oa_ref[:, h * RET_V:(h + 1) * RET_V] = _group_norm_gate(o, g).astype(oa_ref.dtype)
    u = jax.nn.gelu(cols_ref[:, C_BU:C_BU + D_MODEL].astype(F32))
    vn = _ln(jax.nn.gelu(cols_ref[:, C_BV:C_BV + D_MODEL].astype(F32)), lg_ref[...], lb_ref[...])
    vn_ref[...] = vn
    w = _sgu_weights(w_ref)
    for g in range(SGU_GROUPS):
        cs = slice(g * SGU_GW, (g + 1) * SGU_GW)
        wg = w[g][:C, :C]
        f = bs_ref[g][:C, :]
        for j in range(C):
            f = f + wg[:, j:j + 1] * vn[j:j + 1, cs]
        ob_ref[:, cs] = (u[:, cs] * f).astype(ob_ref.dtype)


def _ret_sgu_sample(cols3, state_ret, cosf, sinf, tabs, sgu_ln_g, sgu_ln_b, sgu_w, sgu_b4, layer):
    Bd, C, _ = cols3.shape
    decay, qscale, kscale, gc = tabs
    H = RET_HEADS
    const3 = lambda b: (0, 0, 0)
    return pl.pallas_call(
        functools.partial(_ret_sgu_sample_kernel, C=C),
        grid=(Bd,),
        in_specs=[
            pl.BlockSpec((None, C, N_IN_COLS), lambda b: (b, 0, 0)),
            pl.BlockSpec((None, None, H, RET_QK, RET_V), lambda b: (layer, b, 0, 0, 0)),
            pl.BlockSpec((C, RET_QK), lambda b: (0, 0)),
            pl.BlockSpec((C, RET_QK), lambda b: (0, 0)),
            pl.BlockSpec((H, C, C), const3),
            pl.BlockSpec((H, C, 1), const3),
            pl.BlockSpec((H, C, 1), const3),
            pl.BlockSpec((H, 1, 1), const3),
            pl.BlockSpec((None, 1, D_MODEL), lambda b: (layer, 0, 0)),
            pl.BlockSpec((None, 1, D_MODEL), lambda b: (layer, 0, 0)),
            pl.BlockSpec((None, SGU_GROUPS, SGU_CHUNK, SGU_CHUNK), lambda b: (layer, 0, 0, 0)),
            pl.BlockSpec((None, SGU_GROUPS, SGU_CHUNK, 1), lambda b: (layer, 0, 0, 0)),
        ],
        out_specs=[
            pl.BlockSpec((None, C, D_MODEL), lambda b: (b, 0, 0)),
            pl.BlockSpec((None, C, D_MODEL), lambda b: (b, 0, 0)),
            pl.BlockSpec((None, H, RET_QK, RET_V), lambda b: (b, 0, 0, 0)),
            pl.BlockSpec((None, C, D_MODEL), lambda b: (b, 0, 0)),
        ],
        out_shape=[jax.ShapeDtypeStruct((Bd, C, D_MODEL), BF16),
                   jax.ShapeDtypeStruct((Bd, C, D_MODEL), BF16),
                   jax.ShapeDtypeStruct((Bd, H, RET_QK, RET_V), F32),
                   jax.ShapeDtypeStruct((Bd, C, D_MODEL), F32)],
        compiler_params=_cparams(("parallel",)),
        name="ret_sgu_sample",
    )(cols3, state_ret, cosf, sinf, decay, qscale, kscale, gc, sgu_ln_g, sgu_ln_b, sgu_w, sgu_b4)


def _sample_key_counts(C, w_buf):
    past = np.zeros((C, w_buf), np.float32)
    new = np.zeros((C, C), np.float32)
    for window, dil in DIL_PATTERNS:
        for t in range(C):
            for j in range(window // dil + 1):
                idx = w_buf + t - dil * j
                assert idx >= 0
                if idx < w_buf:
                    past[t, idx] += 1
                else:
                    new[t, idx - w_buf] += 1
    rep = lambda a: np.repeat(a, DIL_HEADS, axis=0)
    return rep(past), rep(new)


def _dil_sample_kernel(q_ref, k_ref, v_ref, kc_ref, vc_ref, cos_ref, sin_ref, cp_ref, cn_ref,
                       o_ref, ko_ref, vo_ref, s_scr, *, C, w_buf):
    H, hd = DIL_HEADS, DIL_HD
    KC = 512
    scale = hd ** -0.5
    cosf, sinf = cos_ref[...], sin_ref[...]
    q_all, k_all = q_ref[...].astype(F32), k_ref[...].astype(F32)
    heads = [slice(h * hd, (h + 1) * hd) for h in range(H)]
    q_rot = jnp.concatenate([_rot(q_all[:, cs], cosf, sinf) for cs in heads], axis=-1)
    k_new = jnp.concatenate([_rot(k_all[:, cs], cosf, sinf) for cs in heads], axis=-1)
    v_new = v_ref[...].astype(F32)
    ko_ref[...] = k_new
    vo_ref[...] = v_new
    head_of_col = lax.broadcasted_iota(jnp.int32, (H, H * hd), 1) // hd
    head_mask = (head_of_col == lax.broadcasted_iota(jnp.int32, (H, H * hd), 0)).astype(F32)
    qbd = jnp.concatenate([jnp.broadcast_to(q_rot[t:t + 1, :], (H, H * hd)) * head_mask for t in range(C)],
                          axis=0).astype(BF16)
    for c in range(w_buf // KC):
        ks = slice(c * KC, (c + 1) * KC)
        s_scr[:, ks] = _dot_nt(qbd, kc_ref[ks, :].astype(BF16)) * scale
    cnt_p, cnt_n = cp_ref[...], cn_ref[...]
    s_p = jnp.where(cnt_p > 0, s_scr[...], -jnp.inf)
    s_n = jnp.where(cnt_n > 0, _dot_nt(qbd, k_new.astype(BF16)) * scale, -jnp.inf)
    m = jnp.maximum(jnp.max(s_p, axis=-1, keepdims=True), jnp.max(s_n, axis=-1, keepdims=True))
    e_p = cnt_p * jnp.exp(s_p - m)
    e_n = cnt_n * jnp.exp(s_n - m)
    l = jnp.sum(e_p, axis=-1, keepdims=True) + jnp.sum(e_n, axis=-1, keepdims=True)
    s_scr[...] = e_p
    acc = e_n[:, 0:1] * v_new[0:1, :]
    for j in range(1, C):
        acc = acc + e_n[:, j:j + 1] * v_new[j:j + 1, :]
    for c in range(w_buf // KC):
        ks = slice(c * KC, (c + 1) * KC)
        acc = acc + _dot(s_scr[:, ks].astype(BF16), vc_ref[ks, :].astype(BF16))
    acc = acc / l
    for t in range(C):
        o_ref[t:t + 1, :] = jnp.sum(acc[t * H:(t + 1) * H, :] * head_mask, axis=0, keepdims=True).astype(o_ref.dtype)


def _dilated_sample(cols3, cache_k, cache_v, cosf, sinf, cnt_p, cnt_n, layer):
    Bd, C, _ = cols3.shape
    w_buf = cache_k.shape[2]
    W = DIL_HEADS * DIL_HD
    colblk = lambda off: (lambda b: (b, 0, off // W))
    const2 = lambda b: (0, 0)
    return pl.pallas_call(
        functools.partial(_dil_sample_kernel, C=C, w_buf=w_buf),
        grid=(Bd,),
        in_specs=[
            pl.BlockSpec((None, C, W), colblk(C_CQ)),
            pl.BlockSpec((None, C, W), colblk(C_CK)),
            pl.BlockSpec((None, C, W), colblk(C_CV)),
            pl.BlockSpec((None, None, w_buf, W), lambda b: (layer, b, 0, 0)),
            pl.BlockSpec((None, None, w_buf, W), lambda b: (layer, b, 0, 0)),
            pl.BlockSpec((C, DIL_HD), const2),
            pl.BlockSpec((C, DIL_HD), const2),
            pl.BlockSpec((C * DIL_HEADS, w_buf), const2),
            pl.BlockSpec((C * DIL_HEADS, C), const2),
        ],
        out_specs=[pl.BlockSpec((None, C, W), lambda b: (b, 0, 0))] * 3,
        out_shape=[jax.ShapeDtypeStruct((Bd, C, W), BF16),
                   jax.ShapeDtypeStruct((Bd, C, W), F32),
                   jax.ShapeDtypeStruct((Bd, C, W), F32)],
        scratch_shapes=[pltpu.VMEM((C * DIL_HEADS, w_buf), F32)],
        compiler_params=_cparams(("parallel",), vmem_mb=48),
        name="dilated_sample",
    )(cols3, cols3, cols3, cache_k, cache_v, cosf, sinf, cnt_p, cnt_n)


def _merge_kernel(a_ref, b_ref, c_ref, g0_ref, g1_ref, g2_ref, x_ref, wb_ref, wo_ref, lg_ref, lb_ref,
                  rw_ref, rb_ref, h_ref, lo_ref, *, alpha):
    merged = None
    for n, (br, gt) in enumerate(((a_ref, g0_ref), (b_ref, g1_ref), (c_ref, g2_ref))):
        term = jax.nn.sigmoid(gt[...].astype(F32)) * _dot(br[...], wb_ref[n])
        merged = term if merged is None else merged + term
    y = _dot(merged.astype(BF16), wo_ref[...])
    h = _ln(alpha * x_ref[...] + y, lg_ref[...], lb_ref[...])
    h_ref[...] = h
    h_hi = h.astype(BF16)
    h_lo = (h - h_hi.astype(F32)).astype(BF16)
    rw = rw_ref[...]
    rw_hi = rw.astype(BF16)
    rw_lo = (rw - rw_hi.astype(F32)).astype(BF16)
    lo_ref[...] = _dot(h_hi, rw_hi) + _dot(h_lo, rw_hi) + _dot(h_hi, rw_lo) + rb_ref[...]


def _merge(out_a, out_b, out_c, cols, x, w_branch_bf, w_out_bf, ln_g, ln_b, router_w, router_b, layer, alpha):
    T = x.shape[0]
    tm = min(T, 256)
    tile = lambda j: (lambda t: (t, j))
    lay2 = lambda t: (layer, 0, 0)
    return pl.pallas_call(
        functools.partial(_merge_kernel, alpha=alpha),
        grid=(T // tm,),
        in_specs=[
            pl.BlockSpec((tm, D_MODEL), tile(0)),
            pl.BlockSpec((tm, D_MODEL), tile(0)),
            pl.BlockSpec((tm, D_MODEL), tile(0)),
            pl.BlockSpec((tm, D_MODEL), tile(C_GATE // D_MODEL)),
            pl.BlockSpec((tm, D_MODEL), tile(C_GATE // D_MODEL + 1)),
            pl.BlockSpec((tm, D_MODEL), tile(C_GATE // D_MODEL + 2)),
            pl.BlockSpec((tm, D_MODEL), tile(0)),
            pl.BlockSpec((None, N_BRANCHES, D_MODEL, D_MODEL), lambda t: (layer, 0, 0, 0)),
            pl.BlockSpec((None, D_MODEL, D_MODEL), lay2),
            pl.BlockSpec((None, 1, D_MODEL), lay2),
            pl.BlockSpec((None, 1, D_MODEL), lay2),
            pl.BlockSpec((None, D_MODEL, N_EXPERTS), lay2),
            pl.BlockSpec((None, 1, N_EXPERTS), lay2),
        ],
        out_specs=[pl.BlockSpec((tm, D_MODEL), tile(0)), pl.BlockSpec((tm, N_EXPERTS), tile(0))],
        out_shape=[jax.ShapeDtypeStruct((T, D_MODEL), F32), jax.ShapeDtypeStruct((T, N_EXPERTS), F32)],
        compiler_params=_cparams(("parallel",), vmem_mb=48),
        name="merge_ln1_router",
    )(out_a, out_b, out_c, cols, cols, cols, x, w_branch_bf, w_out_bf, ln_g, ln_b, router_w, router_b)


def _gather_kernel(nvalid_ref, idx_ref, src_ref, o_ref, sem, *, R):
    j = pl.program_id(0)
    live = j * R < nvalid_ref[0]

    def row_copy(src_row, r):
        return pltpu.make_async_copy(src_ref.at[pl.ds(src_row, 1), :], o_ref.at[pl.ds(r, 1), :], sem)

    @pl.when(live)
    def _():
        def issue(r, carry):
            row_copy(idx_ref[j * R + r], r).start()
            return carry

        lax.fori_loop(0, R, issue, 0, unroll=8)

        def drain(r, carry):
            row_copy(0, r).wait()
            return carry

        lax.fori_loop(0, R, drain, 0, unroll=8)

    @pl.when(jnp.logical_not(live))
    def _():
        o_ref[...] = jnp.zeros_like(o_ref)


def _gather_rows(src, idx, nvalid):
    M = idx.shape[0]
    R = min(GATHER_ROWS, M)
    W = src.shape[1]
    return pl.pallas_call(
        functools.partial(_gather_kernel, R=R),
        grid_spec=pltpu.PrefetchScalarGridSpec(
            num_scalar_prefetch=2,
            grid=(M // R,),
            in_specs=[pl.BlockSpec(memory_space=pl.ANY)],
            out_specs=pl.BlockSpec((R, W), lambda j, nv, ix: (j, 0)),
            scratch_shapes=[pltpu.SemaphoreType.DMA(())],
        ),
        out_shape=jax.ShapeDtypeStruct((M, W), src.dtype),
        compiler_params=_cparams(("arbitrary",)),
        name="gather_rows",
    )(nvalid, idx, src)


def _expert_kernel(be_ref, nu_ref, x_ref, wu_ref, bu_ref, wd_ref, bd_ref, o_ref):
    j = pl.program_id(0)
    live = j < nu_ref[0]

    @pl.when(live)
    def _():
        up = _dot(x_ref[...].astype(BF16), wu_ref[...]) + bu_ref[...]
        glu = jnp.minimum(up[:, :D_FF], SWIGLU_LIMIT)
        lin = jnp.clip(up[:, D_FF:], -SWIGLU_LIMIT, SWIGLU_LIMIT)
        act = glu * jax.nn.sigmoid(SWIGLU_ALPHA * glu) * (lin + 1.0)
        o_ref[...] = _dot(act.astype(BF16), wd_ref[...]) + bd_ref[...]

    @pl.when(jnp.logical_not(live))
    def _():
        o_ref[...] = jnp.zeros_like(o_ref)


def _experts(x_rows, block_expert, n_used, w_up_bf, b_up, w_down_bf, b_down, layer, blk):
    n_rows = x_rows.shape[0]
    return pl.pallas_call(
        _expert_kernel,
        grid_spec=pltpu.PrefetchScalarGridSpec(
            num_scalar_prefetch=2,
            grid=(n_rows // blk,),
            in_specs=[
                pl.BlockSpec((blk, D_MODEL), lambda j, be, nu: (j, 0)),
                pl.BlockSpec((None, None, D_MODEL, 2 * D_FF), lambda j, be, nu: (layer, be[j], 0, 0)),
                pl.BlockSpec((None, None, 1, 2 * D_FF), lambda j, be, nu: (layer, be[j], 0, 0)),
                pl.BlockSpec((None, None, D_FF, D_MODEL), lambda j, be, nu: (layer, be[j], 0, 0)),
                pl.BlockSpec((None, None, 1, D_MODEL), lambda j, be, nu: (layer, be[j], 0, 0)),
            ],
            out_specs=pl.BlockSpec((blk, D_MODEL), lambda j, be, nu: (j, 0)),
        ),
        out_shape=jax.ShapeDtypeStruct((n_rows, D_MODEL), F32),
        compiler_params=_cparams(("arbitrary",), vmem_mb=48),
        name="experts",
    )(block_expert, n_used, x_rows, w_up_bf, b_up, w_down_bf, b_down)


def _final_kernel(h_ref, y_ref, gate_ref, p_ref, wg_ref, wp_ref, lg_ref, lb_ref, o_ref, ob_ref, *, alpha):
    h = h_ref[...]
    ffn = gate_ref[:, 0:1] * y_ref[0]
    for k in range(1, TOP_K):
        ffn = ffn + gate_ref[:, k:k + 1] * y_ref[k]
    ple = jax.nn.sigmoid(_dot(h.astype(BF16), wg_ref[...])) * _dot(p_ref[...].astype(BF16), wp_ref[...])
    out = _ln(alpha * h + ffn + ple, lg_ref[...], lb_ref[...])
    o_ref[...] = out
    ob_ref[...] = out.astype(BF16)


def _final(h, y4, gate, p, ple_w_gate_bf, ple_w_proj_bf, ln_g, ln_b, layer, alpha):
    T = h.shape[0]
    tm = min(T, 256)
    lay2 = lambda t: (layer, 0, 0)
    return pl.pallas_call(
        functools.partial(_final_kernel, alpha=alpha),
        grid=(T // tm,),
        in_specs=[
            pl.BlockSpec((tm, D_MODEL), lambda t: (t, 0)),
            pl.BlockSpec((TOP_K, tm, D_MODEL), lambda t: (0, t, 0)),
            pl.BlockSpec((tm, TOP_K), lambda t: (t, 0)),
            pl.BlockSpec((None, tm, PLE_DIM), lambda t: (layer, t, 0)),
            pl.BlockSpec((None, D_MODEL, D_MODEL), lay2),
            pl.BlockSpec((None, PLE_DIM, D_MODEL), lay2),
            pl.BlockSpec((None, 1, D_MODEL), lay2),
            pl.BlockSpec((None, 1, D_MODEL), lay2),
        ],
        out_specs=[pl.BlockSpec((tm, D_MODEL), lambda t: (t, 0))] * 2,
        out_shape=[jax.ShapeDtypeStruct((T, D_MODEL), F32), jax.ShapeDtypeStruct((T, D_MODEL), BF16)],
        compiler_params=_cparams(("parallel",), vmem_mb=48),
        name="combine_ple_ln2",
    )(h, y4, gate, p, ple_w_gate_bf, ple_w_proj_bf, ln_g, ln_b)


def _moe_route(logits, blk):
    T = logits.shape[0]
    n_pairs = T * TOP_K
    top_logit, top_idx = lax.top_k(logits, TOP_K)
    gate = jax.nn.softmax(top_logit, axis=-1)
    e_flat = top_idx.reshape(n_pairs)
    onehot = (e_flat[:, None] == jnp.arange(N_EXPERTS, dtype=jnp.int32)[None, :]).astype(jnp.int32)
    incl = jnp.cumsum(onehot, axis=0)
    counts = incl[-1]
    rank = jnp.sum((incl - onehot) * onehot, axis=1)
    padded = (counts + blk - 1) // blk * blk
    pad_end = jnp.cumsum(padded)
    pad_start = pad_end - padded
    pos = (pad_start[e_flat] + rank).astype(jnp.int32)
    n_blocks = -(-n_pairs // blk) + N_EXPERTS
    n_rows = n_blocks * blk
    tok = jnp.arange(n_pairs, dtype=jnp.int32) // TOP_K
    row_tok = jnp.zeros((n_rows,), jnp.int32).at[pos].set(tok)
    block_expert = jnp.minimum(
        jnp.searchsorted(pad_end, jnp.arange(n_blocks, dtype=jnp.int32) * blk, side='right'),
        N_EXPERTS - 1).astype(jnp.int32)
    n_used_rows = pad_end[-1:].astype(jnp.int32)
    pos_kmajor = pos.reshape(T, TOP_K).T.reshape(n_pairs)
    return gate, row_tok, block_expert, n_used_rows, pos_kmajor


def _moe_ffn_rows(h, logits, w_up_bf, b_up, w_down_bf, b_down, layer, blk):
    T = h.shape[0]
    gate, row_tok, block_expert, n_used_rows, pos_kmajor = _moe_route(logits, blk)
    x_rows = _gather_rows(h, row_tok, n_used_rows)
    y_rows = _experts(x_rows, block_expert, n_used_rows // blk, w_up_bf, b_up, w_down_bf, b_down, layer, blk)
    y4 = _gather_rows(y_rows, pos_kmajor, jnp.full((1,), T * TOP_K, jnp.int32))
    return y4.reshape(TOP_K, T, D_MODEL), gate


def kernel(x_prompt, x_sample, state_ret, cache_dil_k, cache_dil_v, p_prompt, p_sample, w_in, sgu_ln_g, sgu_ln_b, sgu_w, sgu_b, w_branch, w_out, ln1_g, ln1_b, router_w, router_b, exp_w_up, exp_b_up, exp_w_down, exp_b_down, ple_w_gate, ple_w_proj, ln2_g, ln2_b):
    B, S, D = x_prompt.shape
    Bd, C, _ = x_sample.shape
    depth = w_in.shape[0]
    w_buf = cache_dil_k.shape[2]
    alpha = (2 * depth) ** 0.25
    Tp, Ts = B * S, Bd * C
    W = DIL_HEADS * DIL_HD

    w_in_bf, w_branch_bf, w_out_bf = w_in.astype(BF16), w_branch.astype(BF16), w_out.astype(BF16)
    w_up_bf, w_down_bf = exp_w_up.astype(BF16), exp_w_down.astype(BF16)
    ple_w_gate_bf, ple_w_proj_bf = ple_w_gate.astype(BF16), ple_w_proj.astype(BF16)
    row3 = lambda a: a.reshape(depth, 1, a.shape[-1])
    sgu_ln_g3, sgu_ln_b3 = row3(sgu_ln_g), row3(sgu_ln_b)
    ln1_g3, ln1_b3, ln2_g3, ln2_b3 = row3(ln1_g), row3(ln1_b), row3(ln2_g), row3(ln2_b)
    router_b3 = row3(router_b)
    b_up4 = exp_b_up.reshape(depth, N_EXPERTS, 1, 2 * D_FF)
    b_down4 = exp_b_down.reshape(depth, N_EXPERTS, 1, D_MODEL)
    sgu_b4 = sgu_b.reshape(depth, SGU_GROUPS, SGU_CHUNK, 1)
    p_prompt3 = p_prompt.reshape(depth, Tp, PLE_DIM)
    p_sample3 = p_sample.reshape(depth, Ts, PLE_DIM)
    cache_k4 = cache_dil_k.reshape(depth, Bd, w_buf, W)
    cache_v4 = cache_dil_v.reshape(depth, Bd, w_buf, W)

    pos_p = jnp.arange(S, dtype=jnp.int32)
    pos_s = PAST_LEN + jnp.arange(C, dtype=jnp.int32)
    ret_cos_p, ret_sin_p = _rot_tables(pos_p, _retention_inv_freq())
    ret_cos_s, ret_sin_s = _rot_tables(pos_s, _retention_inv_freq())
    dil_cos_p, dil_sin_p = _rot_tables(pos_p, _rope_inv_freq(DIL_HD))
    dil_cos_s, dil_sin_s = _rot_tables(pos_s, _rope_inv_freq(DIL_HD))
    ret_tabs_p = _retention_tables(RET_CHUNK)
    ret_tabs_s = _retention_tables(C)
    cnt_p, cnt_n = (jnp.asarray(a) for a in _sample_key_counts(C, w_buf))

    xp, xs = x_prompt.reshape(Tp, D), x_sample.reshape(Ts, D)
    xp_bf, xs_bf = xp.astype(BF16), xs.astype(BF16)
    ret_p, ret_s, k_p, v_p, k_s, v_s, sgu_s = [], [], [], [], [], [], []

    def channel_mix(x, out_a, out_b, out_c, cols, p3, i, blk):
        h, logits = _merge(out_a, out_b, out_c, cols, x, w_branch_bf, w_out_bf, ln1_g3, ln1_b3,
                           router_w, router_b3, i, alpha)
        y4, gate = _moe_ffn_rows(h, logits, w_up_bf, b_up4, w_down_bf, b_down4, i, blk)
        return _final(h, y4, gate, p3, ple_w_gate_bf, ple_w_proj_bf, ln2_g3, ln2_b3, i, alpha)

    for i in range(depth):
        cols = _in_proj(xp_bf, w_in_bf, i)
        out_a, st = _retention_prompt(cols, B, S, ret_cos_p, ret_sin_p, ret_tabs_p)
        out_b = _sgu_prompt(cols, Tp, sgu_ln_g3, sgu_ln_b3, sgu_w, sgu_b4, i)
        out_c, kr, vr = _dilated_prompt(cols, B, S, dil_cos_p, dil_sin_p)
        xp, xp_bf = channel_mix(xp, out_a, out_b, out_c, cols, p_prompt3, i, MOE_ROWS_PROMPT)
        ret_p.append(st)
        k_p.append(kr.reshape(B, S, DIL_HEADS, DIL_HD))
        v_p.append(vr.reshape(B, S, DIL_HEADS, DIL_HD))
        cols = _in_proj(xs_bf, w_in_bf, i)
        cols3 = cols.reshape(Bd, C, N_IN_COLS)
        out_a, out_b, st, vn = _ret_sgu_sample(cols3, state_ret, ret_cos_s, ret_sin_s, ret_tabs_s,
                                               sgu_ln_g3, sgu_ln_b3, sgu_w, sgu_b4, i)
        out_c, kr, vr = _dilated_sample(cols3, cache_k4, cache_v4, dil_cos_s, dil_sin_s, cnt_p, cnt_n, i)
        xs, xs_bf = channel_mix(xs, out_a.reshape(Ts, D), out_b.reshape(Ts, D), out_c.reshape(Ts, D), cols,
                                p_sample3, i, MOE_ROWS_SAMPLE)
        ret_s.append(st)
        k_s.append(kr.reshape(Bd, C, DIL_HEADS, DIL_HD))
        v_s.append(vr.reshape(Bd, C, DIL_HEADS, DIL_HD))
        sgu_s.append(vn)

    return (xp.reshape(B, S, D), xs.reshape(Bd, C, D), jnp.stack(ret_p), jnp.stack(ret_s), jnp.stack(k_p),
            jnp.stack(v_p), jnp.stack(k_s), jnp.stack(v_s), jnp.stack(sgu_s))
```

```python
import functools

import numpy as np
import jax
import jax.numpy as jnp
from jax import lax
from jax.experimental import pallas as pl
from jax.experimental.pallas import tpu as pltpu

F32, BF16 = jnp.float32, jnp.bfloat16

D_MODEL = 1024
PAST_LEN = 8192
RET_HEADS, RET_QK, RET_V, RET_CHUNK = 4, 128, 256, 128
SGU_GROUPS, SGU_CHUNK = 4, 128
SGU_GW = D_MODEL // SGU_GROUPS
DIL_HEADS, DIL_HD, DIL_BLOCK = 8, 128, 128
DIL_PATTERNS = ((128, 1), (512, 4), (2048, 16))
ROPE_THETA = 10000.0
N_BRANCHES = 3
N_EXPERTS, TOP_K, D_FF = 32, 4, 1024
SWIGLU_ALPHA, SWIGLU_LIMIT = 1.702, 7.0
PLE_DIM = 256
LN_EPS = 1e-5

C_AQ, C_AK, C_AV, C_AG = 0, 512, 1024, 2048
C_BU, C_BV = 3072, 4096
C_CQ, C_CK, C_CV = 5120, 6144, 7168
C_GATE = 8192
N_IN_COLS = C_GATE + N_BRANCHES * D_MODEL

LANES = 128
SUBLANES = 8
MOE_ROWS_PROMPT = 256
MOE_ROWS_SAMPLE = 128
GATHER_ROWS = 256


def _cparams(sem, vmem_mb=None):
    kw = dict(dimension_semantics=sem)
    if vmem_mb is not None:
        kw["vmem_limit_bytes"] = vmem_mb * 2 ** 20
    return pltpu.CompilerParams(**kw)


def _dot(a, b):
    return jnp.dot(a, b, preferred_element_type=F32)


def _dot_nt(a, b):
    return lax.dot_general(a, b, (((1,), (1,)), ((), ())), preferred_element_type=F32)


def _dot_tn(a, b):
    return lax.dot_general(a, b, (((0,), (0,)), ((), ())), preferred_element_type=F32)


def _ln(x, g, b):
    mu = jnp.mean(x, axis=-1, keepdims=True)
    xc = x - mu
    var = jnp.mean(xc * xc, axis=-1, keepdims=True)
    return xc * lax.rsqrt(var + LN_EPS) * g + b


def _rot(x, cosf, sinf):
    return x * cosf + pltpu.roll(x, LANES // 2, 1) * sinf


def _rot_tables(pos, inv_freq):
    ang = pos.astype(F32)[:, None] * inv_freq[None, :]
    cos, sin = jnp.cos(ang), jnp.sin(ang)
    return jnp.concatenate([cos, cos], axis=-1), jnp.concatenate([-sin, sin], axis=-1)


def _retention_inv_freq():
    return 1.0 / (10000.0 ** jnp.linspace(0.0, 1.0, RET_QK // 2, dtype=F32))


def _rope_inv_freq(hd):
    return ROPE_THETA ** (-jnp.arange(0, hd, 2, dtype=F32) / hd)


def _retention_tables(C):
    log_g = jnp.log1p(-jnp.exp2(-5.0 - jnp.arange(RET_HEADS, dtype=F32)))
    n = jnp.arange(C, dtype=F32)
    diff = n[:, None] - n[None, :]
    decay = jnp.where(diff >= 0, jnp.exp(log_g[:, None, None] * jnp.maximum(diff, 0.0)), 0.0)
    qscale = jnp.exp(log_g[:, None] * (n[None, :] + 1.0))[:, :, None]
    kscale = jnp.exp(log_g[:, None] * (C - 1.0 - n[None, :]))[:, :, None]
    gc = jnp.exp(log_g * C)[:, None, None]
    return decay, qscale, kscale, gc


def _mm_kernel(x_ref, w_ref, o_ref):
    o_ref[...] = _dot(x_ref[...], w_ref[...]).astype(o_ref.dtype)


def _in_proj(x_bf, w_in_bf, layer):
    T = x_bf.shape[0]
    tm = min(T, 512)
    tn = 1024
    return pl.pallas_call(
        _mm_kernel,
        grid=(N_IN_COLS // tn, T // tm),
        in_specs=[pl.BlockSpec((tm, D_MODEL), lambda n, t: (t, 0)),
                  pl.BlockSpec((None, D_MODEL, tn), lambda n, t: (layer, 0, n))],
        out_specs=pl.BlockSpec((tm, tn), lambda n, t: (t, n)),
        out_shape=jax.ShapeDtypeStruct((T, N_IN_COLS), BF16),
        compiler_params=_cparams(("parallel", "arbitrary")),
        name="in_proj",
    )(x_bf, w_in_bf)


def _group_norm_gate(o, g):
    mu = jnp.mean(o, axis=-1, keepdims=True)
    oc = o - mu
    var = jnp.mean(oc * oc, axis=-1, keepdims=True)
    return oc * lax.rsqrt(var + LN_EPS) * jax.nn.silu(g)


def _ret_prompt_kernel(q_ref, k_ref, v_ref, g_ref, cos_ref, sin_ref, decay_ref, qs_ref, ks_ref, gc_ref,
                       o_ref, st_ref, state, *, n_chunks):
    c = pl.program_id(1)

    @pl.when(c == 0)
    def _():
        state[...] = jnp.zeros_like(state)

    cosf, sinf = cos_ref[...], sin_ref[...]
    for h in range(RET_HEADS):
        qk = slice(h * RET_QK, (h + 1) * RET_QK)
        vv = slice(h * RET_V, (h + 1) * RET_V)
        q = _rot(q_ref[:, qk].astype(F32), cosf, sinf)
        k = _rot(k_ref[:, qk].astype(F32), cosf, sinf) * RET_QK ** -0.5
        v = v_ref[:, vv]
        qb = q.astype(BF16)
        s = _dot_nt(qb, k.astype(BF16)) * decay_ref[h]
        o = _dot(s.astype(BF16), v)
        st = state[h]
        o = o + _dot(qb, st.astype(BF16)) * qs_ref[h]
        kw = (k * ks_ref[h]).astype(BF16)
        state[h] = st * gc_ref[h] + _dot_tn(kw, v)
        o_ref[:, vv] = _group_norm_gate(o, g_ref[:, vv].astype(F32)).astype(o_ref.dtype)

    @pl.when(c == n_chunks - 1)
    def _():
        st_ref[...] = state[...]


def _retention_prompt(cols, B, S, cosf, sinf, tabs):
    C = RET_CHUNK
    nc = S // C
    decay, qscale, kscale, gc = tabs
    H = RET_HEADS
    row = lambda b, c: b * nc + c
    return pl.pallas_call(
        functools.partial(_ret_prompt_kernel, n_chunks=nc),
        grid=(B, nc),
        in_specs=[
            pl.BlockSpec((C, H * RET_QK), lambda b, c: (row(b, c), C_AQ // (H * RET_QK))),
            pl.BlockSpec((C, H * RET_QK), lambda b, c: (row(b, c), C_AK // (H * RET_QK))),
            pl.BlockSpec((C, H * RET_V), lambda b, c: (row(b, c), C_AV // (H * RET_V))),
            pl.BlockSpec((C, H * RET_V), lambda b, c: (row(b, c), C_AG // (H * RET_V))),
            pl.BlockSpec((C, RET_QK), lambda b, c: (c, 0)),
            pl.BlockSpec((C, RET_QK), lambda b, c: (c, 0)),
            pl.BlockSpec((H, C, C), lambda b, c: (0, 0, 0)),
            pl.BlockSpec((H, C, 1), lambda b, c: (0, 0, 0)),
            pl.BlockSpec((H, C, 1), lambda b, c: (0, 0, 0)),
            pl.BlockSpec((H, 1, 1), lambda b, c: (0, 0, 0)),
        ],
        out_specs=[
            pl.BlockSpec((C, H * RET_V), lambda b, c: (row(b, c), 0)),
            pl.BlockSpec((None, H, RET_QK, RET_V), lambda b, c: (b, 0, 0, 0)),
        ],
        out_shape=[jax.ShapeDtypeStruct((B * S, H * RET_V), BF16),
                   jax.ShapeDtypeStruct((B, H, RET_QK, RET_V), F32)],
        scratch_shapes=[pltpu.VMEM((H, RET_QK, RET_V), F32)],
        compiler_params=_cparams(("parallel", "arbitrary")),
        name="retention_prompt",
    )(cols, cols, cols, cols, cosf, sinf, decay, qscale, kscale, gc)


def _sgu_weights(w_ref):
    r = lax.broadcasted_iota(jnp.int32, (SGU_CHUNK, SGU_CHUNK), 0)
    c = lax.broadcasted_iota(jnp.int32, (SGU_CHUNK, SGU_CHUNK), 1)
    return [jnp.where(r >= c, w_ref[g], 0.0) for g in range(SGU_GROUPS)]


def _sgu_prompt_kernel(u_ref, v_ref, lg_ref, lb_ref, w_ref, bs_ref, o_ref, *, n_chunks):
    w = [x.astype(BF16) for x in _sgu_weights(w_ref)]
    for c in range(n_chunks):
        rows = slice(c * SGU_CHUNK, (c + 1) * SGU_CHUNK)
        u = jax.nn.gelu(u_ref[rows, :].astype(F32))
        vn = _ln(jax.nn.gelu(v_ref[rows, :].astype(F32)), lg_ref[...], lb_ref[...])
        for g in range(SGU_GROUPS):
            cs = slice(g * SGU_GW, (g + 1) * SGU_GW)
            f = _dot(w[g], vn[:, cs].astype(BF16)) + bs_ref[g]
            o_ref[rows, cs] = (u[:, cs] * f).astype(o_ref.dtype)


def _sgu_prompt(cols, T, sgu_ln_g, sgu_ln_b, sgu_w, sgu_b4, layer):
    n_chunks = 4
    R = n_chunks * SGU_CHUNK
    return pl.pallas_call(
        functools.partial(_sgu_prompt_kernel, n_chunks=n_chunks),
        grid=(T // R,),
        in_specs=[
            pl.BlockSpec((R, D_MODEL), lambda t: (t, C_BU // D_MODEL)),
            pl.BlockSpec((R, D_MODEL), lambda t: (t, C_BV // D_MODEL)),
            pl.BlockSpec((None, 1, D_MODEL), lambda t: (layer, 0, 0)),
            pl.BlockSpec((None, 1, D_MODEL), lambda t: (layer, 0, 0)),
            pl.BlockSpec((None, SGU_GROUPS, SGU_CHUNK, SGU_CHUNK), lambda t: (layer, 0, 0, 0)),
            pl.BlockSpec((None, SGU_GROUPS, SGU_CHUNK, 1), lambda t: (layer, 0, 0, 0)),
        ],
        out_specs=pl.BlockSpec((R, D_MODEL), lambda t: (t, 0)),
        out_shape=jax.ShapeDtypeStruct((T, D_MODEL), BF16),
        compiler_params=_cparams(("parallel",)),
        name="sgu_prompt",
    )(cols, cols, sgu_ln_g, sgu_ln_b, sgu_w, sgu_b4)


def _dil_prompt_kernel(q_ref, k_ref, v_ref, cos_ref, sin_ref, o_ref, ko_ref, vo_ref, qp, kp, vp, o_p, lse_p, *, S):
    BLK, hd = DIL_BLOCK, DIL_HD
    nblk = S // BLK
    scale = hd ** -0.5
    RC = 256

    def rot_body(j, carry):
        r = pl.ds(pl.multiple_of(j * RC, RC), RC)
        cosf, sinf = cos_ref[r, :], sin_ref[r, :]
        k = _rot(k_ref[r, :].astype(F32), cosf, sinf)
        v = v_ref[r, :].astype(F32)
        qp[r, :] = _rot(q_ref[r, :].astype(F32), cosf, sinf)
        kp[r, :] = k
        vp[r, :] = v
        ko_ref[r, :] = k
        vo_ref[r, :] = v
        return carry

    lax.fori_loop(0, S // RC, rot_body, 0)

    qi = lax.broadcasted_iota(jnp.int32, (BLK, BLK), 0)
    kj = lax.broadcasted_iota(jnp.int32, (BLK, BLK), 1)
    lower = kj <= qi
    upper = kj >= qi
    ones_v = jnp.ones((BLK, hd), BF16)

    def block(p, d, j):
        nps = nblk // d
        if d == 1:
            n, start = j, pl.multiple_of(j * BLK, BLK)
            rows = lambda st: pl.ds(pl.multiple_of(st, BLK), BLK)
        else:
            n = j % nps
            start = j // nps + n * (BLK * d)
            rows = lambda st: pl.ds(st, BLK, stride=d)
        v_aug = lambda st: jnp.concatenate([vp[rows(st), :].astype(BF16), ones_v], axis=1)
        qb = qp[rows(start), :].astype(BF16)
        s_c = jnp.where(lower, _dot_nt(qb, kp[rows(start), :].astype(BF16)) * scale, -jnp.inf)
        m = jnp.max(s_c, axis=-1, keepdims=True)
        if nps > 1:
            prev = jnp.where(n > 0, start - BLK * d, start)
            no_prev = jnp.where(n > 0, 0.0, -jnp.inf)
            s_p = jnp.where(upper, _dot_nt(qb, kp[rows(prev), :].astype(BF16)) * scale + no_prev, -jnp.inf)
            m = jnp.maximum(m, jnp.max(s_p, axis=-1, keepdims=True))
        acc = _dot(jnp.exp(s_c - m).astype(BF16), v_aug(start))
        if nps > 1:
            acc = acc + _dot(jnp.exp(s_p - m).astype(BF16), v_aug(prev))
        den = acc[:, hd:]
        o_p[p, rows(start), :] = acc[:, :hd] / den
        lse_p[p, rows(start), :] = m + jnp.log(den)

    def blocks(j, carry):
        for p, (_, d) in enumerate(DIL_PATTERNS):
            block(p, d, j)
        return carry

    lax.fori_loop(0, nblk, blocks, 0, unroll=4)

    def combine(j, carry):
        r = pl.ds(pl.multiple_of(j * BLK, BLK), BLK)
        lses = [lse_p[p, r, :] for p in range(len(DIL_PATTERNS))]
        m = functools.reduce(jnp.maximum, lses)
        ws = [jnp.exp(l - m) for l in lses]
        num = sum(w * o_p[p, r, :] for p, w in enumerate(ws))
        o_ref[r, :] = (num / sum(ws)).astype(o_ref.dtype)
        return carry

    lax.fori_loop(0, nblk, combine, 0)


def _dilated_prompt(cols, B, S, cosf, sinf):
    H, hd = DIL_HEADS, DIL_HD
    P = len(DIL_PATTERNS)
    assert all(w // d == DIL_BLOCK for w, d in DIL_PATTERNS) and DIL_PATTERNS[0][1] == 1
    assert all((S // DIL_BLOCK) % d == 0 for _, d in DIL_PATTERNS)
    col = lambda off: (lambda b, h: (b, off // hd + h))
    vm = lambda shape, dt: pltpu.VMEM(shape, dt)
    return pl.pallas_call(
        functools.partial(_dil_prompt_kernel, S=S),
        grid=(B, H),
        in_specs=[
            pl.BlockSpec((S, hd), col(C_CQ)),
            pl.BlockSpec((S, hd), col(C_CK)),
            pl.BlockSpec((S, hd), col(C_CV)),
            pl.BlockSpec((S, hd), lambda b, h: (0, 0)),
            pl.BlockSpec((S, hd), lambda b, h: (0, 0)),
        ],
        out_specs=[pl.BlockSpec((S, hd), lambda b, h: (b, h))] * 3,
        out_shape=[jax.ShapeDtypeStruct((B * S, H * hd), BF16),
                   jax.ShapeDtypeStruct((B * S, H * hd), F32),
                   jax.ShapeDtypeStruct((B * S, H * hd), F32)],
        scratch_shapes=[vm((S, hd), F32)] * 3 + [vm((P, S, hd), F32), vm((P, S, hd), F32)],
        compiler_params=_cparams(("parallel", "arbitrary"), vmem_mb=40),
        name="dilated_prompt",
    )(cols, cols, cols, cosf, sinf)


def _ret_sgu_sample_kernel(cols_ref, st_ref, cos_ref, sin_ref, decay_ref, qs_ref, ks_ref, gc_ref,
                           lg_ref, lb_ref, w_ref, bs_ref, oa_ref, ob_ref, sto_ref, vn_ref, *, C):
    cosf, sinf = cos_ref[...], sin_ref[...]
    for h in range(RET_HEADS):
        q = _rot(cols_ref[:, C_AQ + h * RET_QK:C_AQ + (h + 1) * RET_QK].astype(F32), cosf, sinf)
        k = _rot(cols_ref[:, C_AK + h * RET_QK:C_AK + (h + 1) * RET_QK].astype(F32), cosf, sinf) * RET_QK ** -0.5
        v = cols_ref[:, C_AV + h * RET_V:C_AV + (h + 1) * RET_V].astype(F32)
        g = cols_ref[:, C_AG + h * RET_V:C_AG + (h + 1) * RET_V].astype(F32)
        s = _dot_nt(q.astype(BF16), k.astype(BF16)) * decay_ref[h]
        o = s[:, 0:1] * v[0:1, :]
        for j in range(1, C):
            o = o + s[:, j:j + 1] * v[j:j + 1, :]
        st = st_ref[h]
        o = o + _dot(q.astype(BF16), st.astype(BF16)) * qs_ref[h]
        kw = (k * ks_ref[h]).astype(BF16)
        sto_ref[h] = st * gc_ref[h] + _dot_tn(kw, v.astype(BF16))
        oa_ref[:, h * RET_V:(h + 1) * RET_V] = _group_norm_gate(o, g).astype(oa_ref.dtype)
    u = jax.nn.gelu(cols_ref[:, C_BU:C_BU + D_MODEL].astype(F32))
    vn = _ln(jax.nn.gelu(cols_ref[:, C_BV:C_BV + D_MODEL].astype(F32)), lg_ref[...], lb_ref[...])
    vn_ref[...] = vn
    w = _sgu_weights(w_ref)
    for g in range(SGU_GROUPS):
        cs = slice(g * SGU_GW, (g + 1) * SGU_GW)
        wg = w[g][:C, :C]
        f = bs_ref[g][:C, :]
        for j in range(C):
            f = f + wg[:, j:j + 1] * vn[j:j + 1, cs]
        ob_ref[:, cs] = (u[:, cs] * f).astype(ob_ref.dtype)


def _ret_sgu_sample(cols3, state_ret, cosf, sinf, tabs, sgu_ln_g, sgu_ln_b, sgu_w, sgu_b4, layer):
    Bd, C, _ = cols3.shape
    decay, qscale, kscale, gc = tabs
    H = RET_HEADS
    const3 = lambda b: (0, 0, 0)
    return pl.pallas_call(
        functools.partial(_ret_sgu_sample_kernel, C=C),
        grid=(Bd,),
        in_specs=[
            pl.BlockSpec((None, C, N_IN_COLS), lambda b: (b, 0, 0)),
            pl.BlockSpec((None, None, H, RET_QK, RET_V), lambda b: (layer, b, 0, 0, 0)),
            pl.BlockSpec((C, RET_QK), lambda b: (0, 0)),
            pl.BlockSpec((C, RET_QK), lambda b: (0, 0)),
            pl.BlockSpec((H, C, C), const3),
            pl.BlockSpec((H, C, 1), const3),
            pl.BlockSpec((H, C, 1), const3),
            pl.BlockSpec((H, 1, 1), const3),
            pl.BlockSpec((None, 1, D_MODEL), lambda b: (layer, 0, 0)),
            pl.BlockSpec((None, 1, D_MODEL), lambda b: (layer, 0, 0)),
            pl.BlockSpec((None, SGU_GROUPS, SGU_CHUNK, SGU_CHUNK), lambda b: (layer, 0, 0, 0)),
            pl.BlockSpec((None, SGU_GROUPS, SGU_CHUNK, 1), lambda b: (layer, 0, 0, 0)),
        ],
        out_specs=[
            pl.BlockSpec((None, C, D_MODEL), lambda b: (b, 0, 0)),
            pl.BlockSpec((None, C, D_MODEL), lambda b: (b, 0, 0)),
            pl.BlockSpec((None, H, RET_QK, RET_V), lambda b: (b, 0, 0, 0)),
            pl.BlockSpec((None, C, D_MODEL), lambda b: (b, 0, 0)),
        ],
        out_shape=[jax.ShapeDtypeStruct((Bd, C, D_MODEL), BF16),
                   jax.ShapeDtypeStruct((Bd, C, D_MODEL), BF16),
                   jax.ShapeDtypeStruct((Bd, H, RET_QK, RET_V), F32),
                   jax.ShapeDtypeStruct((Bd, C, D_MODEL), F32)],
        compiler_params=_cparams(("parallel",)),
        name="ret_sgu_sample",
    )(cols3, state_ret, cosf, sinf, decay, qscale, kscale, gc, sgu_ln_g, sgu_ln_b, sgu_w, sgu_b4)


def _sample_key_tables(C, w_buf):
    H = DIL_HEADS
    per_pattern = []
    for window, dil in DIL_PATTERNS:
        past, new = np.zeros((C, w_buf), np.float32), np.zeros((C, C), np.float32)
        for t in range(C):
            for j in range(window // dil + 1):
                idx = w_buf + t - dil * j
                assert idx >= 0
                if idx < w_buf:
                    past[t, idx] += 1
                else:
                    new[t, idx - w_buf] += 1
        per_pattern.append((past, new))
    tail = max(w for w, _ in DIL_PATTERNS[:-1])
    period = DIL_PATTERNS[-1][1]
    past_tail = sum(p for p, _ in per_pattern[:-1])
    past_wide = per_pattern[-1][0]
    in_wide = (np.arange(w_buf) % period) < C
    assert w_buf % tail == 0 and w_buf % period == 0
    assert not past_tail[:, :w_buf - tail].any() and not past_wide[:, ~in_wide].any()
    new = sum(n for _, n in per_pattern)

    def expand(cnt):
        out = np.zeros((H, C, cnt.shape[1], H), np.float32)
        for h in range(H):
            out[h, :, :, h] = cnt
        return out.reshape(H * C, cnt.shape[1] * H)

    cnt_n = np.zeros((H, C, H, C), np.float32)
    for h in range(H):
        cnt_n[h, :, h, :] = new
    return expand(past_tail[:, w_buf - tail:]), expand(past_wide[:, in_wide]), cnt_n.reshape(H * C, H * C), tail, period


def _dil_sample_kernel(q_ref, k_ref, v_ref, ka_ref, va_ref, kb_ref, vb_ref, cos_ref, sin_ref,
                       ca_ref, cb_ref, cn_ref, o_ref, ko_ref, vo_ref, s_scr, *, C):
    H, hd = DIL_HEADS, DIL_HD
    scale = hd ** -0.5
    KC = 1024
    cosf, sinf = cos_ref[...], sin_ref[...]
    heads = [slice(h * hd, (h + 1) * hd) for h in range(H)]
    q_all, k_all, v_all = q_ref[...].astype(F32), k_ref[...].astype(F32), v_ref[...].astype(F32)
    q_rows = jnp.concatenate([_rot(q_all[:, cs], cosf, sinf) for cs in heads], axis=0)
    k_new = jnp.concatenate([_rot(k_all[:, cs], cosf, sinf) for cs in heads], axis=0)
    v_new = jnp.concatenate([v_all[:, cs] for cs in heads], axis=0)
    ko_ref[...] = k_new
    vo_ref[...] = v_new
    qb = q_rows.astype(BF16)

    na, nb = ka_ref.shape[0] * H // KC, kb_ref.shape[0] * C * H // KC
    ra, rb = KC // H, KC // (C * H)
    chunks = [(ka_ref, va_ref, ca_ref, c, slice(c * ra, (c + 1) * ra)) for c in range(na)]
    chunks += [(kb_ref, vb_ref, cb_ref, c, slice(c * rb, (c + 1) * rb)) for c in range(nb)]

    cnt_n = cn_ref[...]
    s_n = jnp.where(cnt_n > 0, _dot_nt(qb, k_new.astype(BF16)) * scale, -jnp.inf)
    m = jnp.max(s_n, axis=-1, keepdims=True)
    for i, (kref, _, cref, c, rows) in enumerate(chunks):
        cols = slice(c * KC, (c + 1) * KC)
        s = _dot_nt(qb, kref[rows].reshape(KC, hd).astype(BF16)) * scale
        s = jnp.where(cref[:, cols] > 0, s, -jnp.inf)
        s_scr[:, i * KC:(i + 1) * KC] = s
        m = jnp.maximum(m, jnp.max(s, axis=-1, keepdims=True))
    e_n = cnt_n * jnp.exp(s_n - m)
    l = jnp.sum(e_n, axis=-1, keepdims=True)
    acc = _dot(e_n.astype(BF16), v_new.astype(BF16))
    for i, (_, vref, cref, c, rows) in enumerate(chunks):
        cols = slice(c * KC, (c + 1) * KC)
        e = cref[:, cols] * jnp.exp(s_scr[:, i * KC:(i + 1) * KC] - m)
        l = l + jnp.sum(e, axis=-1, keepdims=True)
        acc = acc + _dot(e.astype(BF16), vref[rows].reshape(KC, hd).astype(BF16))
    o_ref[...] = (acc / l).astype(o_ref.dtype)


def _dilated_sample(cols3, cache_k, cache_v, cosf, sinf, tables, layer):
    Bd, C, _ = cols3.shape
    depth, _, w_buf, H, hd = cache_k.shape
    cnt_a, cnt_b, cnt_n, tail, period = tables
    W = H * hd
    tail_view = lambda a: a.reshape(depth, Bd, w_buf // tail, tail, H, hd)
    wide_view = lambda a: a.reshape(depth, Bd, w_buf // period, period, H, hd)
    colblk = lambda off: (lambda b: (b, 0, off // W))
    const2 = lambda b: (0, 0)
    tail_spec = pl.BlockSpec((None, None, None, tail, H, hd), lambda b: (layer, b, w_buf // tail - 1, 0, 0, 0))
    wide_spec = pl.BlockSpec((None, None, w_buf // period, C, H, hd), lambda b: (layer, b, 0, 0, 0, 0))
    n_keys = (tail + w_buf // period * C) * H
    return pl.pallas_call(
        functools.partial(_dil_sample_kernel, C=C),
        grid=(Bd,),
        in_specs=[
            pl.BlockSpec((None, C, W), colblk(C_CQ)),
            pl.BlockSpec((None, C, W), colblk(C_CK)),
            pl.BlockSpec((None, C, W), colblk(C_CV)),
            tail_spec, tail_spec, wide_spec, wide_spec,
            pl.BlockSpec((C, hd), const2),
            pl.BlockSpec((C, hd), const2),
            pl.BlockSpec(cnt_a.shape, const2),
            pl.BlockSpec(cnt_b.shape, const2),
            pl.BlockSpec(cnt_n.shape, const2),
        ],
        out_specs=[pl.BlockSpec((None, H * C, hd), lambda b: (b, 0, 0))] * 3,
        out_shape=[jax.ShapeDtypeStruct((Bd, H * C, hd), BF16),
                   jax.ShapeDtypeStruct((Bd, H * C, hd), F32),
                   jax.ShapeDtypeStruct((Bd, H * C, hd), F32)],
        scratch_shapes=[pltpu.VMEM((H * C, n_keys), F32)],
        compiler_params=_cparams(("parallel",), vmem_mb=40),
        name="dilated_sample",
    )(cols3, cols3, cols3, tail_view(cache_k), tail_view(cache_v), wide_view(cache_k), wide_view(cache_v),
      cosf, sinf, cnt_a, cnt_b, cnt_n)


def _merge_kernel(a_ref, b_ref, c_ref, g0_ref, g1_ref, g2_ref, x_ref, wb_ref, wo_ref, lg_ref, lb_ref,
                  rw_ref, rb_ref, h_ref, lo_ref, *, alpha):
    merged = None
    for n, (br, gt) in enumerate(((a_ref, g0_ref), (b_ref, g1_ref), (c_ref, g2_ref))):
        term = jax.nn.sigmoid(gt[...].astype(F32)) * _dot(br[...], wb_ref[n])
        merged = term if merged is None else merged + term
    y = _dot(merged.astype(BF16), wo_ref[...])
    h = _ln(alpha * x_ref[...] + y, lg_ref[...], lb_ref[...])
    h_ref[...] = h
    h_hi = h.astype(BF16)
    h_lo = (h - h_hi.astype(F32)).astype(BF16)
    rw = rw_ref[...]
    rw_hi = rw.astype(BF16)
    rw_lo = (rw - rw_hi.astype(F32)).astype(BF16)
    lo_ref[...] = _dot(h_hi, rw_hi) + _dot(h_lo, rw_hi) + _dot(h_hi, rw_lo) + rb_ref[...]


def _merge(out_a, out_b, out_c, cols, x, w_branch_bf, w_out_bf, ln_g, ln_b, router_w, router_b, layer, alpha):
    T = x.shape[0]
    tm = min(T, 256)
    tile = lambda j: (lambda t: (t, j))
    lay2 = lambda t: (layer, 0, 0)
    return pl.pallas_call(
        functools.partial(_merge_kernel, alpha=alpha),
        grid=(T // tm,),
        in_specs=[
            pl.BlockSpec((tm, D_MODEL), tile(0)),
            pl.BlockSpec((tm, D_MODEL), tile(0)),
            pl.BlockSpec((tm, D_MODEL), tile(0)),
            pl.BlockSpec((tm, D_MODEL), tile(C_GATE // D_MODEL)),
            pl.BlockSpec((tm, D_MODEL), tile(C_GATE // D_MODEL + 1)),
            pl.BlockSpec((tm, D_MODEL), tile(C_GATE // D_MODEL + 2)),
            pl.BlockSpec((tm, D_MODEL), tile(0)),
            pl.BlockSpec((None, N_BRANCHES, D_MODEL, D_MODEL), lambda t: (layer, 0, 0, 0)),
            pl.BlockSpec((None, D_MODEL, D_MODEL), lay2),
            pl.BlockSpec((None, 1, D_MODEL), lay2),
            pl.BlockSpec((None, 1, D_MODEL), lay2),
            pl.BlockSpec((None, D_MODEL, N_EXPERTS), lay2),
            pl.BlockSpec((None, 1, N_EXPERTS), lay2),
        ],
        out_specs=[pl.BlockSpec((tm, D_MODEL), tile(0)), pl.BlockSpec((tm, N_EXPERTS), tile(0))],
        out_shape=[jax.ShapeDtypeStruct((T, D_MODEL), F32), jax.ShapeDtypeStruct((T, N_EXPERTS), F32)],
        compiler_params=_cparams(("parallel",), vmem_mb=48),
        name="merge_ln1_router",
    )(out_a, out_b, out_c, cols, cols, cols, x, w_branch_bf, w_out_bf, ln_g, ln_b, router_w, router_b)


def _route_kernel(idx_ref, pos_ref, cnt_ref, tot, run, *, blk_log2):
    phase, t = pl.program_id(0), pl.program_id(1)
    tm = idx_ref.shape[0]
    idx = idx_ref[...]
    lane = lax.broadcasted_iota(jnp.int32, (tm, LANES), 1)
    ohs = [(idx[:, k:k + 1] == lane).astype(F32) for k in range(TOP_K)]
    oh = functools.reduce(lambda a, b: a + b, ohs)
    colsum = jnp.sum(oh, axis=0, keepdims=True)

    @pl.when((phase == 0) & (t == 0))
    def _():
        tot[...] = jnp.zeros_like(tot)

    @pl.when(phase == 0)
    def _():
        tot[...] += colsum

    @pl.when((phase == 1) & (t == 0))
    def _():
        counts = tot[...].astype(jnp.int32)
        blk = 1 << blk_log2
        padded = lax.shift_left(lax.shift_right_logical(counts + (blk - 1), blk_log2), blk_log2).astype(F32)
        lane8 = lax.broadcasted_iota(jnp.int32, tot.shape, 1)
        incl = padded
        shift = 1
        while shift < N_EXPERTS:
            incl = incl + jnp.where(lane8 >= shift, pltpu.roll(incl, shift, 1), 0.0)
            shift *= 2
        run[...] = incl - padded

    @pl.when(phase == 1)
    def _():
        r = lax.broadcasted_iota(jnp.int32, (tm, tm), 0)
        c = lax.broadcasted_iota(jnp.int32, (tm, tm), 1)
        before = _dot((c < r).astype(BF16), oh.astype(BF16)) + run[0:1, :]
        pos = jnp.concatenate([jnp.sum(o * before, axis=-1, keepdims=True) for o in ohs], axis=-1)
        pos_ref[...] = pos.astype(jnp.int32)
        run[...] += colsum
        cnt_ref[...] = tot[...].astype(jnp.int32)


def _route_positions(top_idx, blk):
    T = top_idx.shape[0]
    tm = min(T, 512)
    assert blk & (blk - 1) == 0 and N_EXPERTS <= LANES
    pos, cnt = pl.pallas_call(
        functools.partial(_route_kernel, blk_log2=blk.bit_length() - 1),
        grid=(2, T // tm),
        in_specs=[pl.BlockSpec((tm, TOP_K), lambda ph, t: (t, 0))],
        out_specs=[pl.BlockSpec((tm, TOP_K), lambda ph, t: (t * ph, 0)),
                   pl.BlockSpec((SUBLANES, LANES), lambda ph, t: (0, 0))],
        out_shape=[jax.ShapeDtypeStruct((T, TOP_K), jnp.int32),
                   jax.ShapeDtypeStruct((SUBLANES, LANES), jnp.int32)],
        scratch_shapes=[pltpu.VMEM((SUBLANES, LANES), F32)] * 2,
        compiler_params=_cparams(("arbitrary", "arbitrary")),
        name="route_positions",
    )(top_idx)
    return pos, cnt[0, :N_EXPERTS]


def _gather_kernel(nvalid_ref, idx_ref, src_ref, o_ref, buf, sem, *, R):
    j, nsteps = pl.program_id(0), pl.num_programs(0)
    nvalid = nvalid_ref[0]
    slot = j % 2

    def row_copy(src_row, s, r):
        return pltpu.make_async_copy(src_ref.at[pl.ds(src_row, 1), :], buf.at[s, pl.ds(r, 1), :], sem.at[s])

    def issue(step, s):
        def body(g, carry):
            for u in range(SUBLANES):
                r = g * SUBLANES + u
                row_copy(idx_ref[step * R + r], s, r).start()
            return carry

        lax.fori_loop(0, R // SUBLANES, body, 0)

    @pl.when((j == 0) & (nvalid > 0))
    def _():
        issue(0, 0)

    @pl.when((j + 1 < nsteps) & ((j + 1) * R < nvalid))
    def _():
        issue(j + 1, 1 - slot)

    @pl.when(j * R < nvalid)
    def _():
        def drain(r, carry):
            row_copy(0, slot, r).wait()
            return carry

        lax.fori_loop(0, R, drain, 0, unroll=8)
        o_ref[...] = buf[slot]

    @pl.when(j * R >= nvalid)
    def _():
        o_ref[...] = jnp.zeros_like(o_ref)


def _gather_rows(src, idx, nvalid):
    M = idx.shape[0]
    R = min(GATHER_ROWS, M)
    W = src.shape[1]
    return pl.pallas_call(
        functools.partial(_gather_kernel, R=R),
        grid_spec=pltpu.PrefetchScalarGridSpec(
            num_scalar_prefetch=2,
            grid=(M // R,),
            in_specs=[pl.BlockSpec(memory_space=pl.ANY)],
            out_specs=pl.BlockSpec((R, W), lambda j, nv, ix: (j, 0)),
            scratch_shapes=[pltpu.VMEM((2, R, W), src.dtype), pltpu.SemaphoreType.DMA((2,))],
        ),
        out_shape=jax.ShapeDtypeStruct((M, W), src.dtype),
        compiler_params=_cparams(("arbitrary",)),
        name="gather_rows",
    )(nvalid, idx, src)


def _expert_kernel(be_ref, nu_ref, x_ref, wu_ref, bu_ref, wd_ref, bd_ref, o_ref):
    j = pl.program_id(0)
    live = j < nu_ref[0]

    @pl.when(live)
    def _():
        up = _dot(x_ref[...].astype(BF16), wu_ref[...]) + bu_ref[...]
        glu = jnp.minimum(up[:, :D_FF], SWIGLU_LIMIT)
        lin = jnp.clip(up[:, D_FF:], -SWIGLU_LIMIT, SWIGLU_LIMIT)
        act = glu * jax.nn.sigmoid(SWIGLU_ALPHA * glu) * (lin + 1.0)
        o_ref[...] = _dot(act.astype(BF16), wd_ref[...]) + bd_ref[...]

    @pl.when(jnp.logical_not(live))
    def _():
        o_ref[...] = jnp.zeros_like(o_ref)


def _experts(x_rows, block_expert, n_used, w_up_bf, b_up, w_down_bf, b_down, layer, blk):
    n_rows = x_rows.shape[0]
    return pl.pallas_call(
        _expert_kernel,
        grid_spec=pltpu.PrefetchScalarGridSpec(
            num_scalar_prefetch=2,
            grid=(n_rows // blk,),
            in_specs=[
                pl.BlockSpec((blk, D_MODEL), lambda j, be, nu: (j, 0)),
                pl.BlockSpec((None, None, D_MODEL, 2 * D_FF), lambda j, be, nu: (layer, be[j], 0, 0)),
                pl.BlockSpec((None, None, 1, 2 * D_FF), lambda j, be, nu: (layer, be[j], 0, 0)),
                pl.BlockSpec((None, None, D_FF, D_MODEL), lambda j, be, nu: (layer, be[j], 0, 0)),
                pl.BlockSpec((None, None, 1, D_MODEL), lambda j, be, nu: (layer, be[j], 0, 0)),
            ],
            out_specs=pl.BlockSpec((blk, D_MODEL), lambda j, be, nu: (j, 0)),
        ),
        out_shape=jax.ShapeDtypeStruct((n_rows, D_MODEL), F32),
        compiler_params=_cparams(("arbitrary",), vmem_mb=48),
        name="experts",
    )(block_expert, n_used, x_rows, w_up_bf, b_up, w_down_bf, b_down)


def _final_kernel(h_ref, y_ref, gate_ref, p_ref, wg_ref, wp_ref, lg_ref, lb_ref, o_ref, ob_ref, *, alpha):
    h = h_ref[...]
    ffn = gate_ref[:, 0:1] * y_ref[0]
    for k in range(1, TOP_K):
        ffn = ffn + gate_ref[:, k:k + 1] * y_ref[k]
    ple = jax.nn.sigmoid(_dot(h.astype(BF16), wg_ref[...])) * _dot(p_ref[...].astype(BF16), wp_ref[...])
    out = _ln(alpha * h + ffn + ple, lg_ref[...], lb_ref[...])
    o_ref[...] = out
    ob_ref[...] = out.astype(BF16)


def _final(h, y4, gate, p, ple_w_gate_bf, ple_w_proj_bf, ln_g, ln_b, layer, alpha):
    T = h.shape[0]
    tm = min(T, 256)
    lay2 = lambda t: (layer, 0, 0)
    return pl.pallas_call(
        functools.partial(_final_kernel, alpha=alpha),
        grid=(T // tm,),
        in_specs=[
            pl.BlockSpec((tm, D_MODEL), lambda t: (t, 0)),
            pl.BlockSpec((TOP_K, tm, D_MODEL), lambda t: (0, t, 0)),
            pl.BlockSpec((tm, TOP_K), lambda t: (t, 0)),
            pl.BlockSpec((None, tm, PLE_DIM), lambda t: (layer, t, 0)),
            pl.BlockSpec((None, D_MODEL, D_MODEL), lay2),
            pl.BlockSpec((None, PLE_DIM, D_MODEL), lay2),
            pl.BlockSpec((None, 1, D_MODEL), lay2),
            pl.BlockSpec((None, 1, D_MODEL), lay2),
        ],
        out_specs=[pl.BlockSpec((tm, D_MODEL), lambda t: (t, 0))] * 2,
        out_shape=[jax.ShapeDtypeStruct((T, D_MODEL), F32), jax.ShapeDtypeStruct((T, D_MODEL), BF16)],
        compiler_params=_cparams(("parallel",), vmem_mb=48),
        name="combine_ple_ln2",
    )(h, y4, gate, p, ple_w_gate_bf, ple_w_proj_bf, ln_g, ln_b)


def _moe_route(logits, blk):
    T = logits.shape[0]
    n_pairs = T * TOP_K
    top_logit, top_idx = lax.top_k(logits, TOP_K)
    gate = jax.nn.softmax(top_logit, axis=-1)
    pos, counts = _route_positions(top_idx.astype(jnp.int32), blk)
    pos = pos.reshape(n_pairs)
    pad_end = jnp.cumsum((counts + blk - 1) // blk * blk)
    n_blocks = -(-n_pairs // blk) + N_EXPERTS
    n_rows = n_blocks * blk
    tok = jnp.arange(n_pairs, dtype=jnp.int32) // TOP_K
    row_tok = (jnp.arange(n_rows, dtype=jnp.int32) % T).at[pos].set(tok)
    block_expert = jnp.minimum(
        jnp.searchsorted(pad_end, jnp.arange(n_blocks, dtype=jnp.int32) * blk, side='right'),
        N_EXPERTS - 1).astype(jnp.int32)
    n_used_rows = pad_end[-1:].astype(jnp.int32)
    pos_kmajor = pos.reshape(T, TOP_K).T.reshape(n_pairs)
    return gate, row_tok, block_expert, n_used_rows, pos_kmajor


def _moe_ffn_rows(h, logits, w_up_bf, b_up, w_down_bf, b_down, layer, blk):
    T = h.shape[0]
    gate, row_tok, block_expert, n_used_rows, pos_kmajor = _moe_route(logits, blk)
    x_rows = _gather_rows(h, row_tok, n_used_rows)
    y_rows = _experts(x_rows, block_expert, n_used_rows // blk, w_up_bf, b_up, w_down_bf, b_down, layer, blk)
    y4 = _gather_rows(y_rows, pos_kmajor, jnp.full((1,), T * TOP_K, jnp.int32))
    return y4.reshape(TOP_K, T, D_MODEL), gate


def kernel(x_prompt, x_sample, state_ret, cache_dil_k, cache_dil_v, p_prompt, p_sample, w_in, sgu_ln_g, sgu_ln_b, sgu_w, sgu_b, w_branch, w_out, ln1_g, ln1_b, router_w, router_b, exp_w_up, exp_b_up, exp_w_down, exp_b_down, ple_w_gate, ple_w_proj, ln2_g, ln2_b):
    B, S, D = x_prompt.shape
    Bd, C, _ = x_sample.shape
    depth = w_in.shape[0]
    w_buf = cache_dil_k.shape[2]
    alpha = (2 * depth) ** 0.25
    Tp, Ts = B * S, Bd * C
    H, hd = DIL_HEADS, DIL_HD

    w_in_bf, w_branch_bf, w_out_bf = w_in.astype(BF16), w_branch.astype(BF16), w_out.astype(BF16)
    w_up_bf, w_down_bf = exp_w_up.astype(BF16), exp_w_down.astype(BF16)
    ple_w_gate_bf, ple_w_proj_bf = ple_w_gate.astype(BF16), ple_w_proj.astype(BF16)
    row3 = lambda a: a.reshape(depth, 1, a.shape[-1])
    sgu_ln_g3, sgu_ln_b3 = row3(sgu_ln_g), row3(sgu_ln_b)
    ln1_g3, ln1_b3, ln2_g3, ln2_b3 = row3(ln1_g), row3(ln1_b), row3(ln2_g), row3(ln2_b)
    router_b3 = row3(router_b)
    b_up4 = exp_b_up.reshape(depth, N_EXPERTS, 1, 2 * D_FF)
    b_down4 = exp_b_down.reshape(depth, N_EXPERTS, 1, D_MODEL)
    sgu_b4 = sgu_b.reshape(depth, SGU_GROUPS, SGU_CHUNK, 1)
    p_prompt3 = p_prompt.reshape(depth, Tp, PLE_DIM)
    p_sample3 = p_sample.reshape(depth, Ts, PLE_DIM)

    pos_p = jnp.arange(S, dtype=jnp.int32)
    pos_s = PAST_LEN + jnp.arange(C, dtype=jnp.int32)
    ret_cos_p, ret_sin_p = _rot_tables(pos_p, _retention_inv_freq())
    ret_cos_s, ret_sin_s = _rot_tables(pos_s, _retention_inv_freq())
    dil_cos_p, dil_sin_p = _rot_tables(pos_p, _rope_inv_freq(hd))
    dil_cos_s, dil_sin_s = _rot_tables(pos_s, _rope_inv_freq(hd))
    ret_tabs_p = _retention_tables(RET_CHUNK)
    ret_tabs_s = _retention_tables(C)
    cnt_a, cnt_b, cnt_n, tail, period = _sample_key_tables(C, w_buf)
    key_tables = (jnp.asarray(cnt_a), jnp.asarray(cnt_b), jnp.asarray(cnt_n), tail, period)

    xp, xs = x_prompt.reshape(Tp, D), x_sample.reshape(Ts, D)
    xp_bf, xs_bf = xp.astype(BF16), xs.astype(BF16)
    ret_p, ret_s, k_p, v_p, k_s, v_s, sgu_s = [], [], [], [], [], [], []

    def channel_mix(x, out_a, out_b, out_c, cols, p3, i, blk):
        h, logits = _merge(out_a, out_b, out_c, cols, x, w_branch_bf, w_out_bf, ln1_g3, ln1_b3,
                           router_w, router_b3, i, alpha)
        y4, gate = _moe_ffn_rows(h, logits, w_up_bf, b_up4, w_down_bf, b_down4, i, blk)
        return _final(h, y4, gate, p3, ple_w_gate_bf, ple_w_proj_bf, ln2_g3, ln2_b3, i, alpha)

    heads_last = lambda a: a.reshape(Bd, H, C, hd).transpose(0, 2, 1, 3)

    for i in range(depth):
        cols = _in_proj(xp_bf, w_in_bf, i)
        out_a, st = _retention_prompt(cols, B, S, ret_cos_p, ret_sin_p, ret_tabs_p)
        out_b = _sgu_prompt(cols, Tp, sgu_ln_g3, sgu_ln_b3, sgu_w, sgu_b4, i)
        out_c, kr, vr = _dilated_prompt(cols, B, S, dil_cos_p, dil_sin_p)
        xp, xp_bf = channel_mix(xp, out_a, out_b, out_c, cols, p_prompt3, i, MOE_ROWS_PROMPT)
        ret_p.append(st)
        k_p.append(kr.reshape(B, S, H, hd))
        v_p.append(vr.reshape(B, S, H, hd))
        cols = _in_proj(xs_bf, w_in_bf, i)
        cols3 = cols.reshape(Bd, C, N_IN_COLS)
        out_a, out_b, st, vn = _ret_sgu_sample(cols3, state_ret, ret_cos_s, ret_sin_s, ret_tabs_s,
                                               sgu_ln_g3, sgu_ln_b3, sgu_w, sgu_b4, i)
        out_c, kr, vr = _dilated_sample(cols3, cache_dil_k, cache_dil_v, dil_cos_s, dil_sin_s, key_tables, i)
        xs, xs_bf = channel_mix(xs, out_a.reshape(Ts, D), out_b.reshape(Ts, D), heads_last(out_c).reshape(Ts, D),
                                cols, p_sample3, i, MOE_ROWS_SAMPLE)
        ret_s.append(st)
        k_s.append(heads_last(kr))
        v_s.append(heads_last(vr))
        sgu_s.append(vn)

    return (xp.reshape(B, S, D), xs.reshape(Bd, C, D), jnp.stack(ret_p), jnp.stack(ret_s),
            jnp.stack(k_p), jnp.stack(v_p),
            jnp.stack(k_s), jnp.stack(v_s), jnp.stack(sgu_s))
```

```python
import functools

import numpy as np
import jax
import jax.numpy as jnp
from jax import lax
from jax.experimental import pallas as pl
from jax.experimental.pallas import tpu as pltpu

F32, BF16 = jnp.float32, jnp.bfloat16

D_MODEL = 1024
PAST_LEN = 8192
RET_HEADS, RET_QK, RET_V, RET_CHUNK = 4, 128, 256, 128
SGU_GROUPS, SGU_CHUNK = 4, 128
SGU_GW = D_MODEL // SGU_GROUPS
DIL_HEADS, DIL_HD, DIL_BLOCK = 8, 128, 128
DIL_PATTERNS = ((128, 1), (512, 4), (2048, 16))
ROPE_THETA = 10000.0
N_BRANCHES = 3
N_EXPERTS, TOP_K, D_FF = 32, 4, 1024
SWIGLU_ALPHA, SWIGLU_LIMIT = 1.702, 7.0
PLE_DIM = 256
LN_EPS = 1e-5

C_AQ, C_AK, C_AV, C_AG = 0, 512, 1024, 2048
C_BU, C_BV = 3072, 4096
C_CQ, C_CK, C_CV = 5120, 6144, 7168
C_GATE = 8192
N_IN_COLS = C_GATE + N_BRANCHES * D_MODEL

LANES = 128
SUBLANES = 8
MOE_ROWS_PROMPT = 256
MOE_ROWS_SAMPLE = 128
GATHER_ROWS = 256


def _cparams(sem, vmem_mb=None):
    kw = dict(dimension_semantics=sem)
    if vmem_mb is not None:
        kw["vmem_limit_bytes"] = vmem_mb * 2 ** 20
    return pltpu.CompilerParams(**kw)


def _dot(a, b):
    return jnp.dot(a, b, preferred_element_type=F32)


def _dot_nt(a, b):
    return lax.dot_general(a, b, (((1,), (1,)), ((), ())), preferred_element_type=F32)


def _dot_tn(a, b):
    return lax.dot_general(a, b, (((0,), (0,)), ((), ())), preferred_element_type=F32)


def _ln(x, g, b):
    mu = jnp.mean(x, axis=-1, keepdims=True)
    xc = x - mu
    var = jnp.mean(xc * xc, axis=-1, keepdims=True)
    return xc * lax.rsqrt(var + LN_EPS) * g + b


def _rot(x, cosf, sinf):
    return x * cosf + pltpu.roll(x, LANES // 2, 1) * sinf


def _rot_tables(pos, inv_freq):
    ang = pos.astype(F32)[:, None] * inv_freq[None, :]
    cos, sin = jnp.cos(ang), jnp.sin(ang)
    return jnp.concatenate([cos, cos], axis=-1), jnp.concatenate([-sin, sin], axis=-1)


def _retention_inv_freq():
    return 1.0 / (10000.0 ** jnp.linspace(0.0, 1.0, RET_QK // 2, dtype=F32))


def _rope_inv_freq(hd):
    return ROPE_THETA ** (-jnp.arange(0, hd, 2, dtype=F32) / hd)


def _retention_tables(C):
    log_g = jnp.log1p(-jnp.exp2(-5.0 - jnp.arange(RET_HEADS, dtype=F32)))
    n = jnp.arange(C, dtype=F32)
    diff = n[:, None] - n[None, :]
    decay = jnp.where(diff >= 0, jnp.exp(log_g[:, None, None] * jnp.maximum(diff, 0.0)), 0.0)
    qscale = jnp.exp(log_g[:, None] * (n[None, :] + 1.0))[:, :, None]
    kscale = jnp.exp(log_g[:, None] * (C - 1.0 - n[None, :]))[:, :, None]
    gc = jnp.exp(log_g * C)[:, None, None]
    return decay, qscale, kscale, gc


def _mm_kernel(x_ref, w_ref, o_ref):
    o_ref[...] = _dot(x_ref[...], w_ref[...]).astype(o_ref.dtype)


def _in_proj(x_bf, w_in_bf, layer):
    T = x_bf.shape[0]
    tm = min(T, 1024)
    tn = 1024
    return pl.pallas_call(
        _mm_kernel,
        grid=(N_IN_COLS // tn, T // tm),
        in_specs=[pl.BlockSpec((tm, D_MODEL), lambda n, t: (t, 0)),
                  pl.BlockSpec((None, D_MODEL, tn), lambda n, t: (layer, 0, n))],
        out_specs=pl.BlockSpec((tm, tn), lambda n, t: (t, n)),
        out_shape=jax.ShapeDtypeStruct((T, N_IN_COLS), BF16),
        compiler_params=_cparams(("parallel", "arbitrary")),
        name="in_proj",
    )(x_bf, w_in_bf)


def _group_norm_gate(o, g):
    mu = jnp.mean(o, axis=-1, keepdims=True)
    oc = o - mu
    var = jnp.mean(oc * oc, axis=-1, keepdims=True)
    return oc * lax.rsqrt(var + LN_EPS) * jax.nn.silu(g)


def _ret_prompt_kernel(q_ref, k_ref, v_ref, g_ref, cos_ref, sin_ref, decay_ref, qs_ref, ks_ref, gc_ref,
                       o_ref, st_ref, state, *, n_chunks):
    c = pl.program_id(1)

    @pl.when(c == 0)
    def _():
        state[...] = jnp.zeros_like(state)

    cosf, sinf = cos_ref[...], sin_ref[...]
    for h in range(RET_HEADS):
        qk = slice(h * RET_QK, (h + 1) * RET_QK)
        vv = slice(h * RET_V, (h + 1) * RET_V)
        q = _rot(q_ref[:, qk].astype(F32), cosf, sinf)
        k = _rot(k_ref[:, qk].astype(F32), cosf, sinf) * RET_QK ** -0.5
        v = v_ref[:, vv]
        qb = q.astype(BF16)
        s = _dot_nt(qb, k.astype(BF16)) * decay_ref[h]
        o = _dot(s.astype(BF16), v)
        st = state[h]
        o = o + _dot(qb, st.astype(BF16)) * qs_ref[h]
        kw = (k * ks_ref[h]).astype(BF16)
        state[h] = st * gc_ref[h] + _dot_tn(kw, v)
        o_ref[:, vv] = _group_norm_gate(o, g_ref[:, vv].astype(F32)).astype(o_ref.dtype)

    @pl.when(c == n_chunks - 1)
    def _():
        st_ref[...] = state[...]


def _retention_prompt(cols, B, S, cosf, sinf, tabs):
    C = RET_CHUNK
    nc = S // C
    decay, qscale, kscale, gc = tabs
    H = RET_HEADS
    row = lambda b, c: b * nc + c
    return pl.pallas_call(
        functools.partial(_ret_prompt_kernel, n_chunks=nc),
        grid=(B, nc),
        in_specs=[
            pl.BlockSpec((C, H * RET_QK), lambda b, c: (row(b, c), C_AQ // (H * RET_QK))),
            pl.BlockSpec((C, H * RET_QK), lambda b, c: (row(b, c), C_AK // (H * RET_QK))),
            pl.BlockSpec((C, H * RET_V), lambda b, c: (row(b, c), C_AV // (H * RET_V))),
            pl.BlockSpec((C, H * RET_V), lambda b, c: (row(b, c), C_AG // (H * RET_V))),
            pl.BlockSpec((C, RET_QK), lambda b, c: (c, 0)),
            pl.BlockSpec((C, RET_QK), lambda b, c: (c, 0)),
            pl.BlockSpec((H, C, C), lambda b, c: (0, 0, 0)),
            pl.BlockSpec((H, C, 1), lambda b, c: (0, 0, 0)),
            pl.BlockSpec((H, C, 1), lambda b, c: (0, 0, 0)),
            pl.BlockSpec((H, 1, 1), lambda b, c: (0, 0, 0)),
        ],
        out_specs=[
            pl.BlockSpec((C, H * RET_V), lambda b, c: (row(b, c), 0)),
            pl.BlockSpec((None, H, RET_QK, RET_V), lambda b, c: (b, 0, 0, 0)),
        ],
        out_shape=[jax.ShapeDtypeStruct((B * S, H * RET_V), BF16),
                   jax.ShapeDtypeStruct((B, H, RET_QK, RET_V), F32)],
        scratch_shapes=[pltpu.VMEM((H, RET_QK, RET_V), F32)],
        compiler_params=_cparams(("parallel", "arbitrary")),
        name="retention_prompt",
    )(cols, cols, cols, cols, cosf, sinf, decay, qscale, kscale, gc)


def _sgu_weights(w_ref):
    r = lax.broadcasted_iota(jnp.int32, (SGU_CHUNK, SGU_CHUNK), 0)
    c = lax.broadcasted_iota(jnp.int32, (SGU_CHUNK, SGU_CHUNK), 1)
    return [jnp.where(r >= c, w_ref[g], 0.0) for g in range(SGU_GROUPS)]


def _sgu_prompt_kernel(u_ref, v_ref, lg_ref, lb_ref, w_ref, bs_ref, o_ref, *, n_chunks):
    w = [x.astype(BF16) for x in _sgu_weights(w_ref)]
    for c in range(n_chunks):
        rows = slice(c * SGU_CHUNK, (c + 1) * SGU_CHUNK)
        u = jax.nn.gelu(u_ref[rows, :].astype(F32))
        vn = _ln(jax.nn.gelu(v_ref[rows, :].astype(F32)), lg_ref[...], lb_ref[...])
        for g in range(SGU_GROUPS):
            cs = slice(g * SGU_GW, (g + 1) * SGU_GW)
            f = _dot(w[g], vn[:, cs].astype(BF16)) + bs_ref[g]
            o_ref[rows, cs] = (u[:, cs] * f).astype(o_ref.dtype)


def _sgu_prompt(cols, T, sgu_ln_g, sgu_ln_b, sgu_w, sgu_b4, layer):
    n_chunks = 4
    R = n_chunks * SGU_CHUNK
    return pl.pallas_call(
        functools.partial(_sgu_prompt_kernel, n_chunks=n_chunks),
        grid=(T // R,),
        in_specs=[
            pl.BlockSpec((R, D_MODEL), lambda t: (t, C_BU // D_MODEL)),
            pl.BlockSpec((R, D_MODEL), lambda t: (t, C_BV // D_MODEL)),
            pl.BlockSpec((None, 1, D_MODEL), lambda t: (layer, 0, 0)),
            pl.BlockSpec((None, 1, D_MODEL), lambda t: (layer, 0, 0)),
            pl.BlockSpec((None, SGU_GROUPS, SGU_CHUNK, SGU_CHUNK), lambda t: (layer, 0, 0, 0)),
            pl.BlockSpec((None, SGU_GROUPS, SGU_CHUNK, 1), lambda t: (layer, 0, 0, 0)),
        ],
        out_specs=pl.BlockSpec((R, D_MODEL), lambda t: (t, 0)),
        out_shape=jax.ShapeDtypeStruct((T, D_MODEL), BF16),
        compiler_params=_cparams(("parallel",)),
        name="sgu_prompt",
    )(cols, cols, sgu_ln_g, sgu_ln_b, sgu_w, sgu_b4)


def _dil_prompt_kernel(q_ref, k_ref, v_ref, cos_ref, sin_ref, o_ref, ko_ref, vo_ref, qp, kp, vp, o_p, lse_p, *, S):
    BLK, hd = DIL_BLOCK, DIL_HD
    nblk = S // BLK
    scale = hd ** -0.5
    RC = 256

    def rot_body(j, carry):
        r = pl.ds(pl.multiple_of(j * RC, RC), RC)
        cosf, sinf = cos_ref[r, :], sin_ref[r, :]
        k = _rot(k_ref[r, :].astype(F32), cosf, sinf)
        v = v_ref[r, :].astype(F32)
        qp[r, :] = _rot(q_ref[r, :].astype(F32), cosf, sinf)
        kp[r, :] = k
        vp[r, :] = v
        ko_ref[r, :] = k
        vo_ref[r, :] = v
        return carry

    lax.fori_loop(0, S // RC, rot_body, 0)

    qi = lax.broadcasted_iota(jnp.int32, (BLK, BLK), 0)
    kj = lax.broadcasted_iota(jnp.int32, (BLK, BLK), 1)
    lower = kj <= qi
    upper = kj >= qi
    ones_v = jnp.ones((BLK, hd), BF16)

    def block(p, d, j):
        nps = nblk // d
        if d == 1:
            n, start = j, pl.multiple_of(j * BLK, BLK)
            rows = lambda st: pl.ds(pl.multiple_of(st, BLK), BLK)
        else:
            n = j % nps
            start = j // nps + n * (BLK * d)
            rows = lambda st: pl.ds(st, BLK, stride=d)
        v_aug = lambda st: jnp.concatenate([vp[rows(st), :].astype(BF16), ones_v], axis=1)
        qb = qp[rows(start), :].astype(BF16)
        s_c = jnp.where(lower, _dot_nt(qb, kp[rows(start), :].astype(BF16)) * scale, -jnp.inf)
        m = jnp.max(s_c, axis=-1, keepdims=True)
        if nps > 1:
            prev = jnp.where(n > 0, start - BLK * d, start)
            no_prev = jnp.where(n > 0, 0.0, -jnp.inf)
            s_p = jnp.where(upper, _dot_nt(qb, kp[rows(prev), :].astype(BF16)) * scale + no_prev, -jnp.inf)
            m = jnp.maximum(m, jnp.max(s_p, axis=-1, keepdims=True))
        acc = _dot(jnp.exp(s_c - m).astype(BF16), v_aug(start))
        if nps > 1:
            acc = acc + _dot(jnp.exp(s_p - m).astype(BF16), v_aug(prev))
        den = acc[:, hd:]
        o_p[p, rows(start), :] = acc[:, :hd] / den
        lse_p[p, rows(start), :] = m + jnp.log(den)

    def blocks(j, carry):
        for p, (_, d) in enumerate(DIL_PATTERNS):
            block(p, d, j)
        return carry

    lax.fori_loop(0, nblk, blocks, 0, unroll=4)

    def combine(j, carry):
        r = pl.ds(pl.multiple_of(j * BLK, BLK), BLK)
        lses = [lse_p[p, r, :] for p in range(len(DIL_PATTERNS))]
        m = functools.reduce(jnp.maximum, lses)
        ws = [jnp.exp(l - m) for l in lses]
        num = sum(w * o_p[p, r, :] for p, w in enumerate(ws))
        o_ref[r, :] = (num / sum(ws)).astype(o_ref.dtype)
        return carry

    lax.fori_loop(0, nblk, combine, 0)


def _dilated_prompt(cols, B, S, cosf, sinf):
    H, hd = DIL_HEADS, DIL_HD
    P = len(DIL_PATTERNS)
    assert all(w // d == DIL_BLOCK for w, d in DIL_PATTERNS) and DIL_PATTERNS[0][1] == 1
    assert all((S // DIL_BLOCK) % d == 0 for _, d in DIL_PATTERNS)
    col = lambda off: (lambda b, h: (b, off // hd + h))
    vm = lambda shape, dt: pltpu.VMEM(shape, dt)
    return pl.pallas_call(
        functools.partial(_dil_prompt_kernel, S=S),
        grid=(B, H),
        in_specs=[
            pl.BlockSpec((S, hd), col(C_CQ)),
            pl.BlockSpec((S, hd), col(C_CK)),
            pl.BlockSpec((S, hd), col(C_CV)),
            pl.BlockSpec((S, hd), lambda b, h: (0, 0)),
            pl.BlockSpec((S, hd), lambda b, h: (0, 0)),
        ],
        out_specs=[pl.BlockSpec((S, hd), lambda b, h: (b, h))] * 3,
        out_shape=[jax.ShapeDtypeStruct((B * S, H * hd), BF16),
                   jax.ShapeDtypeStruct((B * S, H * hd), F32),
                   jax.ShapeDtypeStruct((B * S, H * hd), F32)],
        scratch_shapes=[vm((S, hd), F32)] * 3 + [vm((P, S, hd), F32), vm((P, S, hd), F32)],
        compiler_params=_cparams(("parallel", "arbitrary"), vmem_mb=40),
        name="dilated_prompt",
    )(cols, cols, cols, cosf, sinf)


def _ret_sgu_sample_kernel(cols_ref, st_ref, cos_ref, sin_ref, decay_ref, qs_ref, ks_ref, gc_ref,
                           lg_ref, lb_ref, w_ref, bs_ref, oa_ref, ob_ref, sto_ref, vn_ref, *, C):
    cosf, sinf = cos_ref[...], sin_ref[...]
    for h in range(RET_HEADS):
        q = _rot(cols_ref[:, C_AQ + h * RET_QK:C_AQ + (h + 1) * RET_QK].astype(F32), cosf, sinf)
        k = _rot(cols_ref[:, C_AK + h * RET_QK:C_AK + (h + 1) * RET_QK].astype(F32), cosf, sinf) * RET_QK ** -0.5
        v = cols_ref[:, C_AV + h * RET_V:C_AV + (h + 1) * RET_V].astype(F32)
        g = cols_ref[:, C_AG + h * RET_V:C_AG + (h + 1) * RET_V].astype(F32)
        s = _dot_nt(q.astype(BF16), k.astype(BF16)) * decay_ref[h]
        o = s[:, 0:1] * v[0:1, :]
        for j in range(1, C):
            o = o + s[:, j:j + 1] * v[j:j + 1, :]
        st = st_ref[h]
        o = o + _dot(q.astype(BF16), st.astype(BF16)) * qs_ref[h]
        kw = (k * ks_ref[h]).astype(BF16)
        sto_ref[h] = st * gc_ref[h] + _dot_tn(kw, v.astype(BF16))
        oa_ref[:, h * RET_V:(h + 1) * RET_V] = _group_norm_gate(o, g).astype(oa_ref.dtype)
    u = jax.nn.gelu(cols_ref[:, C_BU:C_BU + D_MODEL].astype(F32))
    vn = _ln(jax.nn.gelu(cols_ref[:, C_BV:C_BV + D_MODEL].astype(F32)), lg_ref[...], lb_ref[...])
    vn_ref[...] = vn
    w = _sgu_weights(w_ref)
    for g in range(SGU_GROUPS):
        cs = slice(g * SGU_GW, (g + 1) * SGU_GW)
        wg = w[g][:C, :C]
        f = bs_ref[g][:C, :]
        for j in range(C):
            f = f + wg[:, j:j + 1] * vn[j:j + 1, cs]
        ob_ref[:, cs] = (u[:, cs] * f).astype(ob_ref.dtype)


def _ret_sgu_sample(cols3, state_ret, cosf, sinf, tabs, sgu_ln_g, sgu_ln_b, sgu_w, sgu_b4, layer):
    Bd, C, _ = cols3.shape
    decay, qscale, kscale, gc = tabs
    H = RET_HEADS
    const3 = lambda b: (0, 0, 0)
    return pl.pallas_call(
        functools.partial(_ret_sgu_sample_kernel, C=C),
        grid=(Bd,),
        in_specs=[
            pl.BlockSpec((None, C, N_IN_COLS), lambda b: (b, 0, 0)),
            pl.BlockSpec((None, None, H, RET_QK, RET_V), lambda b: (layer, b, 0, 0, 0)),
            pl.BlockSpec((C, RET_QK), lambda b: (0, 0)),
            pl.BlockSpec((C, RET_QK), lambda b: (0, 0)),
            pl.BlockSpec((H, C, C), const3),
            pl.BlockSpec((H, C, 1), const3),
            pl.BlockSpec((H, C, 1), const3),
            pl.BlockSpec((H, 1, 1), const3),
            pl.BlockSpec((None, 1, D_MODEL), lambda b: (layer, 0, 0)),
            pl.BlockSpec((None, 1, D_MODEL), lambda b: (layer, 0, 0)),
            pl.BlockSpec((None, SGU_GROUPS, SGU_CHUNK, SGU_CHUNK), lambda b: (layer, 0, 0, 0)),
            pl.BlockSpec((None, SGU_GROUPS, SGU_CHUNK, 1), lambda b: (layer, 0, 0, 0)),
        ],
        out_specs=[
            pl.BlockSpec((None, C, D_MODEL), lambda b: (b, 0, 0)),
            pl.BlockSpec((None, C, D_MODEL), lambda b: (b, 0, 0)),
            pl.BlockSpec((None, H, RET_QK, RET_V), lambda b: (b, 0, 0, 0)),
            pl.BlockSpec((None, C, D_MODEL), lambda b: (b, 0, 0)),
        ],
        out_shape=[jax.ShapeDtypeStruct((Bd, C, D_MODEL), BF16),
                   jax.ShapeDtypeStruct((Bd, C, D_MODEL), BF16),
                   jax.ShapeDtypeStruct((Bd, H, RET_QK, RET_V), F32),
                   jax.ShapeDtypeStruct((Bd, C, D_MODEL), F32)],
        compiler_params=_cparams(("parallel",)),
        name="ret_sgu_sample",
    )(cols3, state_ret, cosf, sinf, decay, qscale, kscale, gc, sgu_ln_g, sgu_ln_b, sgu_w, sgu_b4)


def _sample_key_tables(C, w_buf):
    H = DIL_HEADS
    per_pattern = []
    for window, dil in DIL_PATTERNS:
        past, new = np.zeros((C, w_buf), np.float32), np.zeros((C, C), np.float32)
        for t in range(C):
            for j in range(window // dil + 1):
                idx = w_buf + t - dil * j
                assert idx >= 0
                if idx < w_buf:
                    past[t, idx] += 1
                else:
                    new[t, idx - w_buf] += 1
        per_pattern.append((past, new))
    tail = max(w for w, _ in DIL_PATTERNS[:-1])
    period = DIL_PATTERNS[-1][1]
    past_tail = sum(p for p, _ in per_pattern[:-1])
    past_wide = per_pattern[-1][0]
    in_wide = (np.arange(w_buf) % period) < C
    assert w_buf % tail == 0 and w_buf % period == 0
    assert not past_tail[:, :w_buf - tail].any() and not past_wide[:, ~in_wide].any()
    new = sum(n for _, n in per_pattern)

    def expand(cnt):
        out = np.zeros((H, C, cnt.shape[1], H), np.float32)
        for h in range(H):
            out[h, :, :, h] = cnt
        return out.reshape(H * C, cnt.shape[1] * H)

    cnt_n = np.zeros((H, C, H, C), np.float32)
    for h in range(H):
        cnt_n[h, :, h, :] = new
    return expand(past_tail[:, w_buf - tail:]), expand(past_wide[:, in_wide]), cnt_n.reshape(H * C, H * C), tail, period


def _dil_sample_kernel(q_ref, k_ref, v_ref, ka_ref, va_ref, kb_ref, vb_ref, cos_ref, sin_ref,
                       ca_ref, cb_ref, cn_ref, o_ref, ko_ref, vo_ref, s_scr, *, C):
    H, hd = DIL_HEADS, DIL_HD
    scale = hd ** -0.5
    KC = 1024
    cosf, sinf = cos_ref[...], sin_ref[...]
    heads = [slice(h * hd, (h + 1) * hd) for h in range(H)]
    q_all, k_all, v_all = q_ref[...].astype(F32), k_ref[...].astype(F32), v_ref[...].astype(F32)
    q_rows = jnp.concatenate([_rot(q_all[:, cs], cosf, sinf) for cs in heads], axis=0)
    k_new = jnp.concatenate([_rot(k_all[:, cs], cosf, sinf) for cs in heads], axis=0)
    v_new = jnp.concatenate([v_all[:, cs] for cs in heads], axis=0)
    ko_ref[...] = k_new
    vo_ref[...] = v_new
    qb = q_rows.astype(BF16)

    na, nb = ka_ref.shape[0] * H // KC, kb_ref.shape[0] * C * H // KC
    ra, rb = KC // H, KC // (C * H)
    chunks = [(ka_ref, va_ref, ca_ref, c, slice(c * ra, (c + 1) * ra)) for c in range(na)]
    chunks += [(kb_ref, vb_ref, cb_ref, c, slice(c * rb, (c + 1) * rb)) for c in range(nb)]

    cnt_n = cn_ref[...]
    s_n = jnp.where(cnt_n > 0, _dot_nt(qb, k_new.astype(BF16)) * scale, -jnp.inf)
    m = jnp.max(s_n, axis=-1, keepdims=True)
    for i, (kref, _, cref, c, rows) in enumerate(chunks):
        cols = slice(c * KC, (c + 1) * KC)
        s = _dot_nt(qb, kref[rows].reshape(KC, hd).astype(BF16)) * scale
        s = jnp.where(cref[:, cols] > 0, s, -jnp.inf)
        s_scr[:, i * KC:(i + 1) * KC] = s
        m = jnp.maximum(m, jnp.max(s, axis=-1, keepdims=True))
    e_n = cnt_n * jnp.exp(s_n - m)
    l = jnp.sum(e_n, axis=-1, keepdims=True)
    acc = _dot(e_n.astype(BF16), v_new.astype(BF16))
    for i, (_, vref, cref, c, rows) in enumerate(chunks):
        cols = slice(c * KC, (c + 1) * KC)
        e = cref[:, cols] * jnp.exp(s_scr[:, i * KC:(i + 1) * KC] - m)
        l = l + jnp.sum(e, axis=-1, keepdims=True)
        acc = acc + _dot(e.astype(BF16), vref[rows].reshape(KC, hd).astype(BF16))
    o_ref[...] = (acc / l).astype(o_ref.dtype)


def _dilated_sample(cols3, cache_k, cache_v, cosf, sinf, tables, layer):
    Bd, C, _ = cols3.shape
    depth, _, w_buf, H, hd = cache_k.shape
    cnt_a, cnt_b, cnt_n, tail, period = tables
    W = H * hd
    tail_view = lambda a: a.reshape(depth, Bd, w_buf // tail, tail, H, hd)
    wide_view = lambda a: a.reshape(depth, Bd, w_buf // period, period, H, hd)
    colblk = lambda off: (lambda b: (b, 0, off // W))
    const2 = lambda b: (0, 0)
    tail_spec = pl.BlockSpec((None, None, None, tail, H, hd), lambda b: (layer, b, w_buf // tail - 1, 0, 0, 0))
    wide_spec = pl.BlockSpec((None, None, w_buf // period, C, H, hd), lambda b: (layer, b, 0, 0, 0, 0))
    n_keys = (tail + w_buf // period * C) * H
    return pl.pallas_call(
        functools.partial(_dil_sample_kernel, C=C),
        grid=(Bd,),
        in_specs=[
            pl.BlockSpec((None, C, W), colblk(C_CQ)),
            pl.BlockSpec((None, C, W), colblk(C_CK)),
            pl.BlockSpec((None, C, W), colblk(C_CV)),
            tail_spec, tail_spec, wide_spec, wide_spec,
            pl.BlockSpec((C, hd), const2),
            pl.BlockSpec((C, hd), const2),
            pl.BlockSpec(cnt_a.shape, const2),
            pl.BlockSpec(cnt_b.shape, const2),
            pl.BlockSpec(cnt_n.shape, const2),
        ],
        out_specs=[pl.BlockSpec((None, H * C, hd), lambda b: (b, 0, 0))] * 3,
        out_shape=[jax.ShapeDtypeStruct((Bd, H * C, hd), BF16),
                   jax.ShapeDtypeStruct((Bd, H * C, hd), F32),
                   jax.ShapeDtypeStruct((Bd, H * C, hd), F32)],
        scratch_shapes=[pltpu.VMEM((H * C, n_keys), F32)],
        compiler_params=_cparams(("parallel",), vmem_mb=40),
        name="dilated_sample",
    )(cols3, cols3, cols3, tail_view(cache_k), tail_view(cache_v), wide_view(cache_k), wide_view(cache_v),
      cosf, sinf, cnt_a, cnt_b, cnt_n)


def _merge_kernel(a_ref, b_ref, c_ref, g0_ref, g1_ref, g2_ref, x_ref, wb_ref, wo_ref, lg_ref, lb_ref,
                  rw_ref, rb_ref, h_ref, idx_ref, gate_ref, *, alpha):
    merged = None
    for n, (br, gt) in enumerate(((a_ref, g0_ref), (b_ref, g1_ref), (c_ref, g2_ref))):
        term = jax.nn.sigmoid(gt[...].astype(F32)) * _dot(br[...], wb_ref[n])
        merged = term if merged is None else merged + term
    y = _dot(merged.astype(BF16), wo_ref[...])
    h = _ln(alpha * x_ref[...] + y, lg_ref[...], lb_ref[...])
    h_ref[...] = h
    h_hi = h.astype(BF16)
    h_lo = (h - h_hi.astype(F32)).astype(BF16)
    rw = rw_ref[...]
    rw_hi = rw.astype(BF16)
    rw_lo = (rw - rw_hi.astype(F32)).astype(BF16)
    logits = _dot(h_hi, rw_hi) + _dot(h_lo, rw_hi) + _dot(h_hi, rw_lo) + rb_ref[...]
    lane = lax.broadcasted_iota(jnp.int32, logits.shape, 1)
    vals, idxs = [], []
    for _ in range(TOP_K):
        mx = jnp.max(logits, axis=-1, keepdims=True)
        am = jnp.min(jnp.where(logits == mx, lane, N_EXPERTS), axis=-1, keepdims=True)
        vals.append(mx)
        idxs.append(am)
        logits = jnp.where(lane == am, -jnp.inf, logits)
    e = jnp.exp(jnp.concatenate(vals, axis=-1) - vals[0])
    idx_ref[...] = jnp.concatenate(idxs, axis=-1)
    gate_ref[...] = e / jnp.sum(e, axis=-1, keepdims=True)


def _merge(out_a, out_b, out_c, cols, x, w_branch_bf, w_out_bf, ln_g, ln_b, router_w, router_b, layer, alpha):
    T = x.shape[0]
    tm = min(T, 256)
    tile = lambda j: (lambda t: (t, j))
    lay2 = lambda t: (layer, 0, 0)
    return pl.pallas_call(
        functools.partial(_merge_kernel, alpha=alpha),
        grid=(T // tm,),
        in_specs=[
            pl.BlockSpec((tm, D_MODEL), tile(0)),
            pl.BlockSpec((tm, D_MODEL), tile(0)),
            pl.BlockSpec((tm, D_MODEL), tile(0)),
            pl.BlockSpec((tm, D_MODEL), tile(C_GATE // D_MODEL)),
            pl.BlockSpec((tm, D_MODEL), tile(C_GATE // D_MODEL + 1)),
            pl.BlockSpec((tm, D_MODEL), tile(C_GATE // D_MODEL + 2)),
            pl.BlockSpec((tm, D_MODEL), tile(0)),
            pl.BlockSpec((None, N_BRANCHES, D_MODEL, D_MODEL), lambda t: (layer, 0, 0, 0)),
            pl.BlockSpec((None, D_MODEL, D_MODEL), lay2),
            pl.BlockSpec((None, 1, D_MODEL), lay2),
            pl.BlockSpec((None, 1, D_MODEL), lay2),
            pl.BlockSpec((None, D_MODEL, N_EXPERTS), lay2),
            pl.BlockSpec((None, 1, N_EXPERTS), lay2),
        ],
        out_specs=[pl.BlockSpec((tm, D_MODEL), tile(0)), pl.BlockSpec((tm, TOP_K), tile(0)),
                   pl.BlockSpec((tm, TOP_K), tile(0))],
        out_shape=[jax.ShapeDtypeStruct((T, D_MODEL), F32), jax.ShapeDtypeStruct((T, TOP_K), jnp.int32),
                   jax.ShapeDtypeStruct((T, TOP_K), F32)],
        compiler_params=_cparams(("parallel",), vmem_mb=48),
        name="merge_ln1_router",
    )(out_a, out_b, out_c, cols, cols, cols, x, w_branch_bf, w_out_bf, ln_g, ln_b, router_w, router_b)


def _route_kernel(idx_ref, pos_ref, cnt_ref, tot, run, *, blk_log2):
    phase, t = pl.program_id(0), pl.program_id(1)
    tm = idx_ref.shape[0]
    idx = idx_ref[...]
    lane = lax.broadcasted_iota(jnp.int32, (tm, LANES), 1)
    ohs = [(idx[:, k:k + 1] == lane).astype(F32) for k in range(TOP_K)]
    oh = functools.reduce(lambda a, b: a + b, ohs)
    colsum = jnp.sum(oh, axis=0, keepdims=True)

    @pl.when((phase == 0) & (t == 0))
    def _():
        tot[...] = jnp.zeros_like(tot)

    @pl.when(phase == 0)
    def _():
        tot[...] += colsum

    @pl.when((phase == 1) & (t == 0))
    def _():
        counts = tot[...].astype(jnp.int32)
        blk = 1 << blk_log2
        padded = lax.shift_left(lax.shift_right_logical(counts + (blk - 1), blk_log2), blk_log2).astype(F32)
        lane8 = lax.broadcasted_iota(jnp.int32, tot.shape, 1)
        incl = padded
        shift = 1
        while shift < N_EXPERTS:
            incl = incl + jnp.where(lane8 >= shift, pltpu.roll(incl, shift, 1), 0.0)
            shift *= 2
        run[...] = incl - padded

    @pl.when(phase == 1)
    def _():
        r = lax.broadcasted_iota(jnp.int32, (tm, tm), 0)
        c = lax.broadcasted_iota(jnp.int32, (tm, tm), 1)
        before = _dot((c < r).astype(BF16), oh.astype(BF16)) + run[0:1, :]
        pos = jnp.concatenate([jnp.sum(o * before, axis=-1, keepdims=True) for o in ohs], axis=-1)
        pos_ref[...] = pos.astype(jnp.int32)
        run[...] += colsum
        cnt_ref[...] = tot[...].astype(jnp.int32)


def _route_positions(top_idx, blk):
    T = top_idx.shape[0]
    tm = min(T, 512)
    assert blk & (blk - 1) == 0 and N_EXPERTS <= LANES
    pos, cnt = pl.pallas_call(
        functools.partial(_route_kernel, blk_log2=blk.bit_length() - 1),
        grid=(2, T // tm),
        in_specs=[pl.BlockSpec((tm, TOP_K), lambda ph, t: (t, 0))],
        out_specs=[pl.BlockSpec((tm, TOP_K), lambda ph, t: (t * ph, 0)),
                   pl.BlockSpec((SUBLANES, LANES), lambda ph, t: (0, 0))],
        out_shape=[jax.ShapeDtypeStruct((T, TOP_K), jnp.int32),
                   jax.ShapeDtypeStruct((SUBLANES, LANES), jnp.int32)],
        scratch_shapes=[pltpu.VMEM((SUBLANES, LANES), F32)] * 2,
        compiler_params=_cparams(("arbitrary", "arbitrary")),
        name="route_positions",
    )(top_idx)
    return pos, cnt[0, :N_EXPERTS]


def _gather_kernel(nvalid_ref, idx_ref, src_ref, o_ref, buf, sem, *, R):
    j, nsteps = pl.program_id(0), pl.num_programs(0)
    nvalid = nvalid_ref[0]
    slot = j % 2

    def row_copy(src_row, s, r):
        return pltpu.make_async_copy(src_ref.at[pl.ds(src_row, 1), :], buf.at[s, pl.ds(r, 1), :], sem.at[s])

    def issue(step, s):
        def body(g, carry):
            for u in range(SUBLANES):
                r = g * SUBLANES + u
                row_copy(idx_ref[step * R + r], s, r).start()
            return carry

        lax.fori_loop(0, R // SUBLANES, body, 0)

    @pl.when((j == 0) & (nvalid > 0))
    def _():
        issue(0, 0)

    @pl.when((j + 1 < nsteps) & ((j + 1) * R < nvalid))
    def _():
        issue(j + 1, 1 - slot)

    @pl.when(j * R < nvalid)
    def _():
        def drain(r, carry):
            row_copy(0, slot, r).wait()
            return carry

        lax.fori_loop(0, R, drain, 0, unroll=8)
        o_ref[...] = buf[slot]

    @pl.when(j * R >= nvalid)
    def _():
        o_ref[...] = jnp.zeros_like(o_ref)


def _gather_rows(src, idx, nvalid):
    M = idx.shape[0]
    R = min(GATHER_ROWS, M)
    W = src.shape[1]
    return pl.pallas_call(
        functools.partial(_gather_kernel, R=R),
        grid_spec=pltpu.PrefetchScalarGridSpec(
            num_scalar_prefetch=2,
            grid=(M // R,),
            in_specs=[pl.BlockSpec(memory_space=pl.ANY)],
            out_specs=pl.BlockSpec((R, W), lambda j, nv, ix: (j, 0)),
            scratch_shapes=[pltpu.VMEM((2, R, W), src.dtype), pltpu.SemaphoreType.DMA((2,))],
        ),
        out_shape=jax.ShapeDtypeStruct((M, W), src.dtype),
        compiler_params=_cparams(("arbitrary",)),
        name="gather_rows",
    )(nvalid, idx, src)


def _expert_kernel(be_ref, nu_ref, rt_ref, h_ref, wu_ref, bu_ref, wd_ref, bd_ref, o_ref, xbuf, xb, sem, *, blk):
    j = pl.program_id(0)
    n_used = nu_ref[0]
    slot = j % 2

    def row_copy(src_row, s, r):
        return pltpu.make_async_copy(h_ref.at[pl.ds(src_row, 1), :], xbuf.at[s, pl.ds(r, 1), :], sem.at[s])

    def drain(s):
        def body(r, carry):
            row_copy(0, s, r).wait()
            return carry

        lax.fori_loop(0, blk, body, 0, unroll=8)

    @pl.when(j == 0)
    def _():
        def body(r, carry):
            row_copy(rt_ref[r], 0, r).start()
            return carry

        lax.fori_loop(0, blk, body, 0, unroll=8)

    @pl.when(j < n_used)
    def _():
        drain(slot)
        xb[...] = xbuf[slot].astype(BF16)
        base = jnp.minimum(j + 1, n_used - 1) * blk
        for r in range(blk):
            row_copy(rt_ref[base + r], 1 - slot, r).start()
        up = _dot(xb[...], wu_ref[...]) + bu_ref[...]
        glu = jnp.minimum(up[:, :D_FF], SWIGLU_LIMIT)
        lin = jnp.clip(up[:, D_FF:], -SWIGLU_LIMIT, SWIGLU_LIMIT)
        act = glu * jax.nn.sigmoid(SWIGLU_ALPHA * glu) * (lin + 1.0)
        o_ref[...] = _dot(act.astype(BF16), wd_ref[...]) + bd_ref[...]

        @pl.when(j == n_used - 1)
        def _():
            drain(1 - slot)

    @pl.when(j >= n_used)
    def _():
        o_ref[...] = jnp.zeros_like(o_ref)


def _experts(h, row_tok, block_expert, n_used, w_up_bf, b_up, w_down_bf, b_down, layer, blk):
    n_rows = row_tok.shape[0]
    wmap = lambda j, be, nu, rt: (layer, be[j], 0, 0)
    return pl.pallas_call(
        functools.partial(_expert_kernel, blk=blk),
        grid_spec=pltpu.PrefetchScalarGridSpec(
            num_scalar_prefetch=3,
            grid=(n_rows // blk,),
            in_specs=[
                pl.BlockSpec(memory_space=pl.ANY),
                pl.BlockSpec((None, None, D_MODEL, 2 * D_FF), wmap),
                pl.BlockSpec((None, None, 1, 2 * D_FF), wmap),
                pl.BlockSpec((None, None, D_FF, D_MODEL), wmap),
                pl.BlockSpec((None, None, 1, D_MODEL), wmap),
            ],
            out_specs=pl.BlockSpec((blk, D_MODEL), lambda j, be, nu, rt: (j, 0)),
            scratch_shapes=[pltpu.VMEM((2, blk, D_MODEL), F32), pltpu.VMEM((blk, D_MODEL), BF16),
                            pltpu.SemaphoreType.DMA((2,))],
        ),
        out_shape=jax.ShapeDtypeStruct((n_rows, D_MODEL), F32),
        compiler_params=_cparams(("arbitrary",), vmem_mb=48),
        name="experts",
    )(block_expert, n_used, row_tok, h, w_up_bf, b_up, w_down_bf, b_down)


def _final_kernel(h_ref, y_ref, gate_ref, p_ref, wg_ref, wp_ref, lg_ref, lb_ref, o_ref, ob_ref, *, alpha):
    h = h_ref[...]
    ffn = gate_ref[:, 0:1] * y_ref[0]
    for k in range(1, TOP_K):
        ffn = ffn + gate_ref[:, k:k + 1] * y_ref[k]
    ple = jax.nn.sigmoid(_dot(h.astype(BF16), wg_ref[...])) * _dot(p_ref[...].astype(BF16), wp_ref[...])
    out = _ln(alpha * h + ffn + ple, lg_ref[...], lb_ref[...])
    o_ref[...] = out
    ob_ref[...] = out.astype(BF16)


def _final(h, y4, gate, p, ple_w_gate_bf, ple_w_proj_bf, ln_g, ln_b, layer, alpha):
    T = h.shape[0]
    tm = min(T, 256)
    lay2 = lambda t: (layer, 0, 0)
    return pl.pallas_call(
        functools.partial(_final_kernel, alpha=alpha),
        grid=(T // tm,),
        in_specs=[
            pl.BlockSpec((tm, D_MODEL), lambda t: (t, 0)),
            pl.BlockSpec((TOP_K, tm, D_MODEL), lambda t: (0, t, 0)),
            pl.BlockSpec((tm, TOP_K), lambda t: (t, 0)),
            pl.BlockSpec((None, tm, PLE_DIM), lambda t: (layer, t, 0)),
            pl.BlockSpec((None, D_MODEL, D_MODEL), lay2),
            pl.BlockSpec((None, PLE_DIM, D_MODEL), lay2),
            pl.BlockSpec((None, 1, D_MODEL), lay2),
            pl.BlockSpec((None, 1, D_MODEL), lay2),
        ],
        out_specs=[pl.BlockSpec((tm, D_MODEL), lambda t: (t, 0))] * 2,
        out_shape=[jax.ShapeDtypeStruct((T, D_MODEL), F32), jax.ShapeDtypeStruct((T, D_MODEL), BF16)],
        compiler_params=_cparams(("parallel",), vmem_mb=48),
        name="combine_ple_ln2",
    )(h, y4, gate, p, ple_w_gate_bf, ple_w_proj_bf, ln_g, ln_b)


def _moe_route(top_idx, blk):
    T = top_idx.shape[0]
    n_pairs = T * TOP_K
    pos, counts = _route_positions(top_idx, blk)
    pos = pos.reshape(n_pairs)
    pad_end = jnp.cumsum((counts + blk - 1) // blk * blk)
    n_blocks = -(-n_pairs // blk) + N_EXPERTS
    n_rows = n_blocks * blk
    tok = jnp.arange(n_pairs, dtype=jnp.int32) // TOP_K
    row_tok = (jnp.arange(n_rows, dtype=jnp.int32) % T).at[pos].set(
        tok, unique_indices=True, mode='promise_in_bounds')
    block_expert = jnp.minimum(
        jnp.searchsorted(pad_end, jnp.arange(n_blocks, dtype=jnp.int32) * blk, side='right'),
        N_EXPERTS - 1).astype(jnp.int32)
    n_used_rows = pad_end[-1:].astype(jnp.int32)
    pos_kmajor = pos.reshape(T, TOP_K).T.reshape(n_pairs)
    return row_tok, block_expert, n_used_rows, pos_kmajor


def _moe_ffn_rows(h, top_idx, w_up_bf, b_up, w_down_bf, b_down, layer, blk):
    T = h.shape[0]
    row_tok, block_expert, n_used_rows, pos_kmajor = _moe_route(top_idx, blk)
    y_rows = _experts(h, row_tok, block_expert, n_used_rows // blk, w_up_bf, b_up, w_down_bf, b_down, layer, blk)
    y4 = _gather_rows(y_rows, pos_kmajor, jnp.full((1,), T * TOP_K, jnp.int32))
    return y4.reshape(TOP_K, T, D_MODEL)


def kernel(x_prompt, x_sample, state_ret, cache_dil_k, cache_dil_v, p_prompt, p_sample, w_in, sgu_ln_g, sgu_ln_b, sgu_w, sgu_b, w_branch, w_out, ln1_g, ln1_b, router_w, router_b, exp_w_up, exp_b_up, exp_w_down, exp_b_down, ple_w_gate, ple_w_proj, ln2_g, ln2_b):
    B, S, D = x_prompt.shape
    Bd, C, _ = x_sample.shape
    depth = w_in.shape[0]
    w_buf = cache_dil_k.shape[2]
    alpha = (2 * depth) ** 0.25
    Tp, Ts = B * S, Bd * C
    H, hd = DIL_HEADS, DIL_HD

    w_in_bf, w_branch_bf, w_out_bf = w_in.astype(BF16), w_branch.astype(BF16), w_out.astype(BF16)
    w_up_bf, w_down_bf = exp_w_up.astype(BF16), exp_w_down.astype(BF16)
    ple_w_gate_bf, ple_w_proj_bf = ple_w_gate.astype(BF16), ple_w_proj.astype(BF16)
    row3 = lambda a: a.reshape(depth, 1, a.shape[-1])
    sgu_ln_g3, sgu_ln_b3 = row3(sgu_ln_g), row3(sgu_ln_b)
    ln1_g3, ln1_b3, ln2_g3, ln2_b3 = row3(ln1_g), row3(ln1_b), row3(ln2_g), row3(ln2_b)
    router_b3 = row3(router_b)
    b_up4 = exp_b_up.reshape(depth, N_EXPERTS, 1, 2 * D_FF)
    b_down4 = exp_b_down.reshape(depth, N_EXPERTS, 1, D_MODEL)
    sgu_b4 = sgu_b.reshape(depth, SGU_GROUPS, SGU_CHUNK, 1)
    p_prompt3 = p_prompt.reshape(depth, Tp, PLE_DIM)
    p_sample3 = p_sample.reshape(depth, Ts, PLE_DIM)

    pos_p = jnp.arange(S, dtype=jnp.int32)
    pos_s = PAST_LEN + jnp.arange(C, dtype=jnp.int32)
    ret_cos_p, ret_sin_p = _rot_tables(pos_p, _retention_inv_freq())
    ret_cos_s, ret_sin_s = _rot_tables(pos_s, _retention_inv_freq())
    dil_cos_p, dil_sin_p = _rot_tables(pos_p, _rope_inv_freq(hd))
    dil_cos_s, dil_sin_s = _rot_tables(pos_s, _rope_inv_freq(hd))
    ret_tabs_p = _retention_tables(RET_CHUNK)
    ret_tabs_s = _retention_tables(C)
    cnt_a, cnt_b, cnt_n, tail, period = _sample_key_tables(C, w_buf)
    key_tables = (jnp.asarray(cnt_a), jnp.asarray(cnt_b), jnp.asarray(cnt_n), tail, period)

    xp, xs = x_prompt.reshape(Tp, D), x_sample.reshape(Ts, D)
    xp_bf, xs_bf = xp.astype(BF16), xs.astype(BF16)
    ret_p, ret_s, k_p, v_p, k_s, v_s, sgu_s = [], [], [], [], [], [], []

    def channel_mix(x, out_a, out_b, out_c, cols, p3, i, blk):
        h, top_idx, gate = _merge(out_a, out_b, out_c, cols, x, w_branch_bf, w_out_bf, ln1_g3, ln1_b3,
                                  router_w, router_b3, i, alpha)
        y4 = _moe_ffn_rows(h, top_idx, w_up_bf, b_up4, w_down_bf, b_down4, i, blk)
        return _final(h, y4, gate, p3, ple_w_gate_bf, ple_w_proj_bf, ln2_g3, ln2_b3, i, alpha)

    heads_last = lambda a: a.reshape(Bd, H, C, hd).transpose(0, 2, 1, 3)

    for i in range(depth):
        cols = _in_proj(xp_bf, w_in_bf, i)
        out_a, st = _retention_prompt(cols, B, S, ret_cos_p, ret_sin_p, ret_tabs_p)
        out_b = _sgu_prompt(cols, Tp, sgu_ln_g3, sgu_ln_b3, sgu_w, sgu_b4, i)
        out_c, kr, vr = _dilated_prompt(cols, B, S, dil_cos_p, dil_sin_p)
        xp, xp_bf = channel_mix(xp, out_a, out_b, out_c, cols, p_prompt3, i, MOE_ROWS_PROMPT)
        ret_p.append(st)
        k_p.append(kr.reshape(B, S, H, hd))
        v_p.append(vr.reshape(B, S, H, hd))
        cols = _in_proj(xs_bf, w_in_bf, i)
        cols3 = cols.reshape(Bd, C, N_IN_COLS)
        out_a, out_b, st, vn = _ret_sgu_sample(cols3, state_ret, ret_cos_s, ret_sin_s, ret_tabs_s,
                                               sgu_ln_g3, sgu_ln_b3, sgu_w, sgu_b4, i)
        out_c, kr, vr = _dilated_sample(cols3, cache_dil_k, cache_dil_v, dil_cos_s, dil_sin_s, key_tables, i)
        xs, xs_bf = channel_mix(xs, out_a.reshape(Ts, D), out_b.reshape(Ts, D), heads_last(out_c).reshape(Ts, D),
                                cols, p_sample3, i, MOE_ROWS_SAMPLE)
        ret_s.append(st)
        k_s.append(heads_last(kr))
        v_s.append(heads_last(vr))
        sgu_s.append(vn)

    return (xp.reshape(B, S, D), xs.reshape(Bd, C, D), jnp.stack(ret_p), jnp.stack(ret_s),
            jnp.stack(k_p), jnp.stack(v_p),
            jnp.stack(k_s), jnp.stack(v_s), jnp.stack(sgu_s))
```

```python
import functools

import numpy as np
import jax
import jax.numpy as jnp
from jax import lax
from jax.experimental import pallas as pl
from jax.experimental.pallas import tpu as pltpu

F32, BF16 = jnp.float32, jnp.bfloat16

D_MODEL = 1024
PAST_LEN = 8192
RET_HEADS, RET_QK, RET_V, RET_CHUNK = 4, 128, 256, 128
SGU_GROUPS, SGU_CHUNK = 4, 128
SGU_GW = D_MODEL // SGU_GROUPS
DIL_HEADS, DIL_HD, DIL_BLOCK = 8, 128, 128
DIL_PATTERNS = ((128, 1), (512, 4), (2048, 16))
ROPE_THETA = 10000.0
N_BRANCHES = 3
N_EXPERTS, TOP_K, D_FF = 32, 4, 1024
SWIGLU_ALPHA, SWIGLU_LIMIT = 1.702, 7.0
PLE_DIM = 256
LN_EPS = 1e-5

C_AQ, C_AK, C_AV, C_AG = 0, 512, 1024, 2048
C_BU, C_BV = 3072, 4096
C_CQ, C_CK, C_CV = 5120, 6144, 7168
C_GATE = 8192
N_IN_COLS = C_GATE + N_BRANCHES * D_MODEL

LANES = 128
SUBLANES = 8
MOE_ROWS_PROMPT = 256
MOE_ROWS_SAMPLE = 128


def _cparams(sem, vmem_mb=None):
    kw = dict(dimension_semantics=sem)
    if vmem_mb is not None:
        kw["vmem_limit_bytes"] = vmem_mb * 2 ** 20
    return pltpu.CompilerParams(**kw)


def _dot(a, b):
    return jnp.dot(a, b, preferred_element_type=F32)


def _dot_nt(a, b):
    return lax.dot_general(a, b, (((1,), (1,)), ((), ())), preferred_element_type=F32)


def _dot_tn(a, b):
    return lax.dot_general(a, b, (((0,), (0,)), ((), ())), preferred_element_type=F32)


def _ln(x, g, b):
    mu = jnp.mean(x, axis=-1, keepdims=True)
    xc = x - mu
    var = jnp.mean(xc * xc, axis=-1, keepdims=True)
    return xc * lax.rsqrt(var + LN_EPS) * g + b


def _rot(x, cosf, sinf):
    return x * cosf + pltpu.roll(x, LANES // 2, 1) * sinf


def _rot_tables(pos, inv_freq):
    ang = pos.astype(F32)[:, None] * inv_freq[None, :]
    cos, sin = jnp.cos(ang), jnp.sin(ang)
    return jnp.concatenate([cos, cos], axis=-1), jnp.concatenate([-sin, sin], axis=-1)


def _retention_inv_freq():
    return 1.0 / (10000.0 ** jnp.linspace(0.0, 1.0, RET_QK // 2, dtype=F32))


def _rope_inv_freq(hd):
    return ROPE_THETA ** (-jnp.arange(0, hd, 2, dtype=F32) / hd)


def _retention_tables(C):
    log_g = jnp.log1p(-jnp.exp2(-5.0 - jnp.arange(RET_HEADS, dtype=F32)))
    n = jnp.arange(C, dtype=F32)
    diff = n[:, None] - n[None, :]
    decay = jnp.where(diff >= 0, jnp.exp(log_g[:, None, None] * jnp.maximum(diff, 0.0)), 0.0)
    qscale = jnp.exp(log_g[:, None] * (n[None, :] + 1.0))[:, :, None]
    kscale = jnp.exp(log_g[:, None] * (C - 1.0 - n[None, :]))[:, :, None]
    gc = jnp.exp(log_g * C)[:, None, None]
    return decay, qscale, kscale, gc


def _mm_kernel(x_ref, w_ref, o_ref):
    o_ref[...] = _dot(x_ref[...], w_ref[...]).astype(o_ref.dtype)


def _in_proj(x_bf, w_in_bf, layer):
    T = x_bf.shape[0]
    tm = min(T, 2048)
    tn = 1024
    return pl.pallas_call(
        _mm_kernel,
        grid=(N_IN_COLS // tn, T // tm),
        in_specs=[pl.BlockSpec((tm, D_MODEL), lambda n, t: (t, 0)),
                  pl.BlockSpec((None, D_MODEL, tn), lambda n, t: (layer, 0, n))],
        out_specs=pl.BlockSpec((tm, tn), lambda n, t: (t, n)),
        out_shape=jax.ShapeDtypeStruct((T, N_IN_COLS), BF16),
        compiler_params=_cparams(("parallel", "arbitrary")),
        name="in_proj",
    )(x_bf, w_in_bf)


def _group_norm_gate(o, g):
    mu = jnp.mean(o, axis=-1, keepdims=True)
    oc = o - mu
    var = jnp.mean(oc * oc, axis=-1, keepdims=True)
    return oc * lax.rsqrt(var + LN_EPS) * jax.nn.silu(g)


def _ret_prompt_kernel(q_ref, k_ref, v_ref, g_ref, cos_ref, sin_ref, decay_ref, qs_ref, ks_ref, gc_ref,
                       o_ref, st_ref, state, *, n_chunks):
    c = pl.program_id(1)

    @pl.when(c == 0)
    def _():
        state[...] = jnp.zeros_like(state)

    cosf, sinf = cos_ref[...], sin_ref[...]
    for h in range(RET_HEADS):
        qk = slice(h * RET_QK, (h + 1) * RET_QK)
        vv = slice(h * RET_V, (h + 1) * RET_V)
        q = _rot(q_ref[:, qk].astype(F32), cosf, sinf)
        k = _rot(k_ref[:, qk].astype(F32), cosf, sinf) * RET_QK ** -0.5
        v = v_ref[:, vv]
        qb = q.astype(BF16)
        s = _dot_nt(qb, k.astype(BF16)) * decay_ref[h]
        o = _dot(s.astype(BF16), v)
        st = state[h]
        o = o + _dot(qb, st.astype(BF16)) * qs_ref[h]
        kw = (k * ks_ref[h]).astype(BF16)
        state[h] = st * gc_ref[h] + _dot_tn(kw, v)
        o_ref[:, vv] = _group_norm_gate(o, g_ref[:, vv].astype(F32)).astype(o_ref.dtype)

    @pl.when(c == n_chunks - 1)
    def _():
        st_ref[...] = state[...]


def _retention_prompt(cols, B, S, cosf, sinf, tabs):
    C = RET_CHUNK
    nc = S // C
    decay, qscale, kscale, gc = tabs
    H = RET_HEADS
    row = lambda b, c: b * nc + c
    return pl.pallas_call(
        functools.partial(_ret_prompt_kernel, n_chunks=nc),
        grid=(B, nc),
        in_specs=[
            pl.BlockSpec((C, H * RET_QK), lambda b, c: (row(b, c), C_AQ // (H * RET_QK))),
            pl.BlockSpec((C, H * RET_QK), lambda b, c: (row(b, c), C_AK // (H * RET_QK))),
            pl.BlockSpec((C, H * RET_V), lambda b, c: (row(b, c), C_AV // (H * RET_V))),
            pl.BlockSpec((C, H * RET_V), lambda b, c: (row(b, c), C_AG // (H * RET_V))),
            pl.BlockSpec((C, RET_QK), lambda b, c: (c, 0)),
            pl.BlockSpec((C, RET_QK), lambda b, c: (c, 0)),
            pl.BlockSpec((H, C, C), lambda b, c: (0, 0, 0)),
            pl.BlockSpec((H, C, 1), lambda b, c: (0, 0, 0)),
            pl.BlockSpec((H, C, 1), lambda b, c: (0, 0, 0)),
            pl.BlockSpec((H, 1, 1), lambda b, c: (0, 0, 0)),
        ],
        out_specs=[
            pl.BlockSpec((C, H * RET_V), lambda b, c: (row(b, c), 0)),
            pl.BlockSpec((None, H, RET_QK, RET_V), lambda b, c: (b, 0, 0, 0)),
        ],
        out_shape=[jax.ShapeDtypeStruct((B * S, H * RET_V), BF16),
                   jax.ShapeDtypeStruct((B, H, RET_QK, RET_V), F32)],
        scratch_shapes=[pltpu.VMEM((H, RET_QK, RET_V), F32)],
        compiler_params=_cparams(("parallel", "arbitrary")),
        name="retention_prompt",
    )(cols, cols, cols, cols, cosf, sinf, decay, qscale, kscale, gc)


def _sgu_weights(w_ref):
    r = lax.broadcasted_iota(jnp.int32, (SGU_CHUNK, SGU_CHUNK), 0)
    c = lax.broadcasted_iota(jnp.int32, (SGU_CHUNK, SGU_CHUNK), 1)
    return [jnp.where(r >= c, w_ref[g], 0.0) for g in range(SGU_GROUPS)]


def _sgu_prompt_kernel(u_ref, v_ref, lg_ref, lb_ref, w_ref, bs_ref, o_ref, *, n_chunks):
    w = [x.astype(BF16) for x in _sgu_weights(w_ref)]
    for c in range(n_chunks):
        rows = slice(c * SGU_CHUNK, (c + 1) * SGU_CHUNK)
        u = jax.nn.gelu(u_ref[rows, :].astype(F32))
        vn = _ln(jax.nn.gelu(v_ref[rows, :].astype(F32)), lg_ref[...], lb_ref[...])
        for g in range(SGU_GROUPS):
            cs = slice(g * SGU_GW, (g + 1) * SGU_GW)
            f = _dot(w[g], vn[:, cs].astype(BF16)) + bs_ref[g]
            o_ref[rows, cs] = (u[:, cs] * f).astype(o_ref.dtype)


def _sgu_prompt(cols, T, sgu_ln_g, sgu_ln_b, sgu_w, sgu_b4, layer):
    n_chunks = 4
    R = n_chunks * SGU_CHUNK
    return pl.pallas_call(
        functools.partial(_sgu_prompt_kernel, n_chunks=n_chunks),
        grid=(T // R,),
        in_specs=[
            pl.BlockSpec((R, D_MODEL), lambda t: (t, C_BU // D_MODEL)),
            pl.BlockSpec((R, D_MODEL), lambda t: (t, C_BV // D_MODEL)),
            pl.BlockSpec((None, 1, D_MODEL), lambda t: (layer, 0, 0)),
            pl.BlockSpec((None, 1, D_MODEL), lambda t: (layer, 0, 0)),
            pl.BlockSpec((None, SGU_GROUPS, SGU_CHUNK, SGU_CHUNK), lambda t: (layer, 0, 0, 0)),
            pl.BlockSpec((None, SGU_GROUPS, SGU_CHUNK, 1), lambda t: (layer, 0, 0, 0)),
        ],
        out_specs=pl.BlockSpec((R, D_MODEL), lambda t: (t, 0)),
        out_shape=jax.ShapeDtypeStruct((T, D_MODEL), BF16),
        compiler_params=_cparams(("parallel",)),
        name="sgu_prompt",
    )(cols, cols, sgu_ln_g, sgu_ln_b, sgu_w, sgu_b4)


def _dil_prompt_kernel(q_ref, k_ref, v_ref, cos_ref, sin_ref, o_ref, ko_ref, vo_ref, qp, kp, vp, o_p, lse_p, *, S):
    BLK, hd = DIL_BLOCK, DIL_HD
    nblk = S // BLK
    scale = hd ** -0.5
    RC = 256

    def rot_body(j, carry):
        r = pl.ds(pl.multiple_of(j * RC, RC), RC)
        cosf, sinf = cos_ref[r, :], sin_ref[r, :]
        k = _rot(k_ref[r, :].astype(F32), cosf, sinf)
        v = v_ref[r, :].astype(F32)
        qp[r, :] = _rot(q_ref[r, :].astype(F32), cosf, sinf)
        kp[r, :] = k
        vp[r, :] = v
        ko_ref[r, :] = k
        vo_ref[r, :] = v
        return carry

    lax.fori_loop(0, S // RC, rot_body, 0)

    qi = lax.broadcasted_iota(jnp.int32, (BLK, BLK), 0)
    kj = lax.broadcasted_iota(jnp.int32, (BLK, BLK), 1)
    lower = kj <= qi
    upper = kj >= qi
    ones_v = jnp.ones((BLK, hd), BF16)

    def block(p, d, j):
        nps = nblk // d
        if d == 1:
            n, start = j, pl.multiple_of(j * BLK, BLK)
            rows = lambda st: pl.ds(pl.multiple_of(st, BLK), BLK)
        else:
            n = j % nps
            start = j // nps + n * (BLK * d)
            rows = lambda st: pl.ds(st, BLK, stride=d)
        v_aug = lambda st: jnp.concatenate([vp[rows(st), :].astype(BF16), ones_v], axis=1)
        qb = qp[rows(start), :].astype(BF16)
        s_c = jnp.where(lower, _dot_nt(qb, kp[rows(start), :].astype(BF16)) * scale, -jnp.inf)
        m = jnp.max(s_c, axis=-1, keepdims=True)
        if nps > 1:
            prev = jnp.where(n > 0, start - BLK * d, start)
            no_prev = jnp.where(n > 0, 0.0, -jnp.inf)
            s_p = jnp.where(upper, _dot_nt(qb, kp[rows(prev), :].astype(BF16)) * scale + no_prev, -jnp.inf)
            m = jnp.maximum(m, jnp.max(s_p, axis=-1, keepdims=True))
        acc = _dot(jnp.exp(s_c - m).astype(BF16), v_aug(start))
        if nps > 1:
            acc = acc + _dot(jnp.exp(s_p - m).astype(BF16), v_aug(prev))
        den = acc[:, hd:]
        o_p[p, rows(start), :] = acc[:, :hd] / den
        lse_p[p, rows(start), :] = m + jnp.log(den)

    def blocks(j, carry):
        for p, (_, d) in enumerate(DIL_PATTERNS):
            block(p, d, j)
        return carry

    lax.fori_loop(0, nblk, blocks, 0, unroll=4)

    def combine(j, carry):
        r = pl.ds(pl.multiple_of(j * BLK, BLK), BLK)
        lses = [lse_p[p, r, :] for p in range(len(DIL_PATTERNS))]
        m = functools.reduce(jnp.maximum, lses)
        ws = [jnp.exp(l - m) for l in lses]
        num = sum(w * o_p[p, r, :] for p, w in enumerate(ws))
        o_ref[r, :] = (num / sum(ws)).astype(o_ref.dtype)
        return carry

    lax.fori_loop(0, nblk, combine, 0)


def _dilated_prompt(cols, B, S, cosf, sinf):
    H, hd = DIL_HEADS, DIL_HD
    P = len(DIL_PATTERNS)
    assert all(w // d == DIL_BLOCK for w, d in DIL_PATTERNS) and DIL_PATTERNS[0][1] == 1
    assert all((S // DIL_BLOCK) % d == 0 for _, d in DIL_PATTERNS)
    col = lambda off: (lambda b, h: (b, off // hd + h))
    vm = lambda shape, dt: pltpu.VMEM(shape, dt)
    return pl.pallas_call(
        functools.partial(_dil_prompt_kernel, S=S),
        grid=(B, H),
        in_specs=[
            pl.BlockSpec((S, hd), col(C_CQ)),
            pl.BlockSpec((S, hd), col(C_CK)),
            pl.BlockSpec((S, hd), col(C_CV)),
            pl.BlockSpec((S, hd), lambda b, h: (0, 0)),
            pl.BlockSpec((S, hd), lambda b, h: (0, 0)),
        ],
        out_specs=[pl.BlockSpec((S, hd), lambda b, h: (b, h))] * 3,
        out_shape=[jax.ShapeDtypeStruct((B * S, H * hd), BF16),
                   jax.ShapeDtypeStruct((B * S, H * hd), F32),
                   jax.ShapeDtypeStruct((B * S, H * hd), F32)],
        scratch_shapes=[vm((S, hd), F32)] * 3 + [vm((P, S, hd), F32), vm((P, S, hd), F32)],
        compiler_params=_cparams(("parallel", "arbitrary"), vmem_mb=40),
        name="dilated_prompt",
    )(cols, cols, cols, cosf, sinf)


def _ret_sgu_sample_kernel(cols_ref, st_ref, cos_ref, sin_ref, decay_ref, qs_ref, ks_ref, gc_ref,
                           lg_ref, lb_ref, w_ref, bs_ref, oa_ref, ob_ref, sto_ref, vn_ref, *, C):
    cosf, sinf = cos_ref[...], sin_ref[...]
    for h in range(RET_HEADS):
        q = _rot(cols_ref[:, C_AQ + h * RET_QK:C_AQ + (h + 1) * RET_QK].astype(F32), cosf, sinf)
        k = _rot(cols_ref[:, C_AK + h * RET_QK:C_AK + (h + 1) * RET_QK].astype(F32), cosf, sinf) * RET_QK ** -0.5
        v = cols_ref[:, C_AV + h * RET_V:C_AV + (h + 1) * RET_V].astype(F32)
        g = cols_ref[:, C_AG + h * RET_V:C_AG + (h + 1) * RET_V].astype(F32)
        s = _dot_nt(q.astype(BF16), k.astype(BF16)) * decay_ref[h]
        o = s[:, 0:1] * v[0:1, :]
        for j in range(1, C):
            o = o + s[:, j:j + 1] * v[j:j + 1, :]
        st = st_ref[h]
        o = o + _dot(q.astype(BF16), st.astype(BF16)) * qs_ref[h]
        kw = (k * ks_ref[h]).astype(BF16)
        sto_ref[h] = st * gc_ref[h] + _dot_tn(kw, v.astype(BF16))
        oa_ref[:, h * RET_V:(h + 1) * RET_V] = _group_norm_gate(o, g).astype(oa_ref.dtype)
    u = jax.nn.gelu(cols_ref[:, C_BU:C_BU + D_MODEL].astype(F32))
    vn = _ln(jax.nn.gelu(cols_ref[:, C_BV:C_BV + D_MODEL].astype(F32)), lg_ref[...], lb_ref[...])
    vn_ref[...] = vn
    w = _sgu_weights(w_ref)
    for g in range(SGU_GROUPS):
        cs = slice(g * SGU_GW, (g + 1) * SGU_GW)
        wg = w[g][:C, :C]
        f = bs_ref[g][:C, :]
        for j in range(C):
            f = f + wg[:, j:j + 1] * vn[j:j + 1, cs]
        ob_ref[:, cs] = (u[:, cs] * f).astype(ob_ref.dtype)


def _ret_sgu_sample(cols3, state_ret, cosf, sinf, tabs, sgu_ln_g, sgu_ln_b, sgu_w, sgu_b4, layer):
    Bd, C, _ = cols3.shape
    decay, qscale, kscale, gc = tabs
    H = RET_HEADS
    const3 = lambda b: (0, 0, 0)
    return pl.pallas_call(
        functools.partial(_ret_sgu_sample_kernel, C=C),
        grid=(Bd,),
        in_specs=[
            pl.BlockSpec((None, C, N_IN_COLS), lambda b: (b, 0, 0)),
            pl.BlockSpec((None, None, H, RET_QK, RET_V), lambda b: (layer, b, 0, 0, 0)),
            pl.BlockSpec((C, RET_QK), lambda b: (0, 0)),
            pl.BlockSpec((C, RET_QK), lambda b: (0, 0)),
            pl.BlockSpec((H, C, C), const3),
            pl.BlockSpec((H, C, 1), const3),
            pl.BlockSpec((H, C, 1), const3),
            pl.BlockSpec((H, 1, 1), const3),
            pl.BlockSpec((None, 1, D_MODEL), lambda b: (layer, 0, 0)),
            pl.BlockSpec((None, 1, D_MODEL), lambda b: (layer, 0, 0)),
            pl.BlockSpec((None, SGU_GROUPS, SGU_CHUNK, SGU_CHUNK), lambda b: (layer, 0, 0, 0)),
            pl.BlockSpec((None, SGU_GROUPS, SGU_CHUNK, 1), lambda b: (layer, 0, 0, 0)),
        ],
        out_specs=[
            pl.BlockSpec((None, C, D_MODEL), lambda b: (b, 0, 0)),
            pl.BlockSpec((None, C, D_MODEL), lambda b: (b, 0, 0)),
            pl.BlockSpec((None, H, RET_QK, RET_V), lambda b: (b, 0, 0, 0)),
            pl.BlockSpec((None, C, D_MODEL), lambda b: (b, 0, 0)),
        ],
        out_shape=[jax.ShapeDtypeStruct((Bd, C, D_MODEL), BF16),
                   jax.ShapeDtypeStruct((Bd, C, D_MODEL), BF16),
                   jax.ShapeDtypeStruct((Bd, H, RET_QK, RET_V), F32),
                   jax.ShapeDtypeStruct((Bd, C, D_MODEL), F32)],
        compiler_params=_cparams(("parallel",)),
        name="ret_sgu_sample",
    )(cols3, state_ret, cosf, sinf, decay, qscale, kscale, gc, sgu_ln_g, sgu_ln_b, sgu_w, sgu_b4)


def _sample_key_tables(C, w_buf):
    H = DIL_HEADS
    per_pattern = []
    for window, dil in DIL_PATTERNS:
        past, new = np.zeros((C, w_buf), np.float32), np.zeros((C, C), np.float32)
        for t in range(C):
            for j in range(window // dil + 1):
                idx = w_buf + t - dil * j
                assert idx >= 0
                if idx < w_buf:
                    past[t, idx] += 1
                else:
                    new[t, idx - w_buf] += 1
        per_pattern.append((past, new))
    tail = max(w for w, _ in DIL_PATTERNS[:-1])
    period = DIL_PATTERNS[-1][1]
    past_tail = sum(p for p, _ in per_pattern[:-1])
    past_wide = per_pattern[-1][0]
    in_wide = (np.arange(w_buf) % period) < C
    assert w_buf % tail == 0 and w_buf % period == 0
    assert not past_tail[:, :w_buf - tail].any() and not past_wide[:, ~in_wide].any()
    new = sum(n for _, n in per_pattern)

    def expand(cnt):
        out = np.zeros((H, C, cnt.shape[1], H), np.float32)
        for h in range(H):
            out[h, :, :, h] = cnt
        return out.reshape(H * C, cnt.shape[1] * H)

    cnt_n = np.zeros((H, C, H, C), np.float32)
    for h in range(H):
        cnt_n[h, :, h, :] = new
    return expand(past_tail[:, w_buf - tail:]), expand(past_wide[:, in_wide]), cnt_n.reshape(H * C, H * C), tail, period


def _dil_sample_kernel(q_ref, k_ref, v_ref, ka_ref, va_ref, kb_ref, vb_ref, cos_ref, sin_ref,
                       ca_ref, cb_ref, cn_ref, o_ref, ko_ref, vo_ref, s_scr, *, C):
    H, hd = DIL_HEADS, DIL_HD
    scale = hd ** -0.5
    KC = 1024
    cosf, sinf = cos_ref[...], sin_ref[...]
    heads = [slice(h * hd, (h + 1) * hd) for h in range(H)]
    q_all, k_all, v_all = q_ref[...].astype(F32), k_ref[...].astype(F32), v_ref[...].astype(F32)
    q_rows = jnp.concatenate([_rot(q_all[:, cs], cosf, sinf) for cs in heads], axis=0)
    k_new = jnp.concatenate([_rot(k_all[:, cs], cosf, sinf) for cs in heads], axis=0)
    v_new = jnp.concatenate([v_all[:, cs] for cs in heads], axis=0)
    ko_ref[...] = k_new
    vo_ref[...] = v_new
    qb = q_rows.astype(BF16)

    na, nb = ka_ref.shape[0] * H // KC, kb_ref.shape[0] * C * H // KC
    ra, rb = KC // H, KC // (C * H)
    chunks = [(ka_ref, va_ref, ca_ref, c, slice(c * ra, (c + 1) * ra)) for c in range(na)]
    chunks += [(kb_ref, vb_ref, cb_ref, c, slice(c * rb, (c + 1) * rb)) for c in range(nb)]

    cnt_n = cn_ref[...]
    s_n = jnp.where(cnt_n > 0, _dot_nt(qb, k_new.astype(BF16)) * scale, -jnp.inf)
    m = jnp.max(s_n, axis=-1, keepdims=True)
    for i, (kref, _, cref, c, rows) in enumerate(chunks):
        cols = slice(c * KC, (c + 1) * KC)
        s = _dot_nt(qb, kref[rows].reshape(KC, hd).astype(BF16)) * scale
        s = jnp.where(cref[:, cols] > 0, s, -jnp.inf)
        s_scr[:, i * KC:(i + 1) * KC] = s
        m = jnp.maximum(m, jnp.max(s, axis=-1, keepdims=True))
    e_n = cnt_n * jnp.exp(s_n - m)
    l = jnp.sum(e_n, axis=-1, keepdims=True)
    acc = _dot(e_n.astype(BF16), v_new.astype(BF16))
    for i, (_, vref, cref, c, rows) in enumerate(chunks):
        cols = slice(c * KC, (c + 1) * KC)
        e = cref[:, cols] * jnp.exp(s_scr[:, i * KC:(i + 1) * KC] - m)
        l = l + jnp.sum(e, axis=-1, keepdims=True)
        acc = acc + _dot(e.astype(BF16), vref[rows].reshape(KC, hd).astype(BF16))
    o_ref[...] = (acc / l).astype(o_ref.dtype)


def _dilated_sample(cols3, cache_k, cache_v, cosf, sinf, tables, layer):
    Bd, C, _ = cols3.shape
    depth, _, w_buf, H, hd = cache_k.shape
    cnt_a, cnt_b, cnt_n, tail, period = tables
    W = H * hd
    tail_view = lambda a: a.reshape(depth, Bd, w_buf // tail, tail, H, hd)
    wide_view = lambda a: a.reshape(depth, Bd, w_buf // period, period, H, hd)
    colblk = lambda off: (lambda b: (b, 0, off // W))
    const2 = lambda b: (0, 0)
    tail_spec = pl.BlockSpec((None, None, None, tail, H, hd), lambda b: (layer, b, w_buf // tail - 1, 0, 0, 0))
    wide_spec = pl.BlockSpec((None, None, w_buf // period, C, H, hd), lambda b: (layer, b, 0, 0, 0, 0))
    n_keys = (tail + w_buf // period * C) * H
    return pl.pallas_call(
        functools.partial(_dil_sample_kernel, C=C),
        grid=(Bd,),
        in_specs=[
            pl.BlockSpec((None, C, W), colblk(C_CQ)),
            pl.BlockSpec((None, C, W), colblk(C_CK)),
            pl.BlockSpec((None, C, W), colblk(C_CV)),
            tail_spec, tail_spec, wide_spec, wide_spec,
            pl.BlockSpec((C, hd), const2),
            pl.BlockSpec((C, hd), const2),
            pl.BlockSpec(cnt_a.shape, const2),
            pl.BlockSpec(cnt_b.shape, const2),
            pl.BlockSpec(cnt_n.shape, const2),
        ],
        out_specs=[pl.BlockSpec((None, H * C, hd), lambda b: (b, 0, 0))] * 3,
        out_shape=[jax.ShapeDtypeStruct((Bd, H * C, hd), BF16),
                   jax.ShapeDtypeStruct((Bd, H * C, hd), F32),
                   jax.ShapeDtypeStruct((Bd, H * C, hd), F32)],
        scratch_shapes=[pltpu.VMEM((H * C, n_keys), F32)],
        compiler_params=_cparams(("parallel",), vmem_mb=40),
        name="dilated_sample",
    )(cols3, cols3, cols3, tail_view(cache_k), tail_view(cache_v), wide_view(cache_k), wide_view(cache_v),
      cosf, sinf, cnt_a, cnt_b, cnt_n)


def _merge_kernel(a_ref, b_ref, c_ref, g0_ref, g1_ref, g2_ref, x_ref, wb_ref, wo_ref, lg_ref, lb_ref,
                  rw_ref, rb_ref, h_ref, idx_ref, gate_ref, *, alpha):
    merged = None
    for n, (br, gt) in enumerate(((a_ref, g0_ref), (b_ref, g1_ref), (c_ref, g2_ref))):
        term = jax.nn.sigmoid(gt[...].astype(F32)) * _dot(br[...], wb_ref[n])
        merged = term if merged is None else merged + term
    y = _dot(merged.astype(BF16), wo_ref[...])
    h = _ln(alpha * x_ref[...] + y, lg_ref[...], lb_ref[...])
    h_ref[...] = h
    h_hi = h.astype(BF16)
    h_lo = (h - h_hi.astype(F32)).astype(BF16)
    rw = rw_ref[...]
    rw_hi = rw.astype(BF16)
    rw_lo = (rw - rw_hi.astype(F32)).astype(BF16)
    logits = _dot(h_hi, rw_hi) + _dot(h_lo, rw_hi) + _dot(h_hi, rw_lo) + rb_ref[...]
    lane = lax.broadcasted_iota(jnp.int32, logits.shape, 1)
    vals, idxs = [], []
    for _ in range(TOP_K):
        mx = jnp.max(logits, axis=-1, keepdims=True)
        am = jnp.min(jnp.where(logits == mx, lane, N_EXPERTS), axis=-1, keepdims=True)
        vals.append(mx)
        idxs.append(am)
        logits = jnp.where(lane == am, -jnp.inf, logits)
    e = jnp.exp(jnp.concatenate(vals, axis=-1) - vals[0])
    idx_ref[...] = jnp.concatenate(idxs, axis=-1)
    gate_ref[...] = e / jnp.sum(e, axis=-1, keepdims=True)


def _merge(out_a, out_b, out_c, cols, x, w_branch_bf, w_out_bf, ln_g, ln_b, router_w, router_b, layer, alpha):
    T = x.shape[0]
    tm = min(T, 256)
    tile = lambda j: (lambda t: (t, j))
    lay2 = lambda t: (layer, 0, 0)
    return pl.pallas_call(
        functools.partial(_merge_kernel, alpha=alpha),
        grid=(T // tm,),
        in_specs=[
            pl.BlockSpec((tm, D_MODEL), tile(0)),
            pl.BlockSpec((tm, D_MODEL), tile(0)),
            pl.BlockSpec((tm, D_MODEL), tile(0)),
            pl.BlockSpec((tm, D_MODEL), tile(C_GATE // D_MODEL)),
            pl.BlockSpec((tm, D_MODEL), tile(C_GATE // D_MODEL + 1)),
            pl.BlockSpec((tm, D_MODEL), tile(C_GATE // D_MODEL + 2)),
            pl.BlockSpec((tm, D_MODEL), tile(0)),
            pl.BlockSpec((None, N_BRANCHES, D_MODEL, D_MODEL), lambda t: (layer, 0, 0, 0)),
            pl.BlockSpec((None, D_MODEL, D_MODEL), lay2),
            pl.BlockSpec((None, 1, D_MODEL), lay2),
            pl.BlockSpec((None, 1, D_MODEL), lay2),
            pl.BlockSpec((None, D_MODEL, N_EXPERTS), lay2),
            pl.BlockSpec((None, 1, N_EXPERTS), lay2),
        ],
        out_specs=[pl.BlockSpec((tm, D_MODEL), tile(0)), pl.BlockSpec((tm, TOP_K), tile(0)),
                   pl.BlockSpec((tm, TOP_K), tile(0))],
        out_shape=[jax.ShapeDtypeStruct((T, D_MODEL), F32), jax.ShapeDtypeStruct((T, TOP_K), jnp.int32),
                   jax.ShapeDtypeStruct((T, TOP_K), F32)],
        compiler_params=_cparams(("parallel",), vmem_mb=48),
        name="merge_ln1_router",
    )(out_a, out_b, out_c, cols, cols, cols, x, w_branch_bf, w_out_bf, ln_g, ln_b, router_w, router_b)


def _route_kernel(idx_ref, pos_ref, cnt_ref, tot, run, *, blk_log2):
    phase, t = pl.program_id(0), pl.program_id(1)
    tm = idx_ref.shape[0]
    idx = idx_ref[...]
    lane = lax.broadcasted_iota(jnp.int32, (tm, LANES), 1)
    ohs = [(idx[:, k:k + 1] == lane).astype(F32) for k in range(TOP_K)]
    oh = functools.reduce(lambda a, b: a + b, ohs)
    colsum = jnp.sum(oh, axis=0, keepdims=True)

    @pl.when((phase == 0) & (t == 0))
    def _():
        tot[...] = jnp.zeros_like(tot)

    @pl.when(phase == 0)
    def _():
        tot[...] += colsum

    @pl.when((phase == 1) & (t == 0))
    def _():
        counts = tot[...].astype(jnp.int32)
        blk = 1 << blk_log2
        padded = lax.shift_left(lax.shift_right_logical(counts + (blk - 1), blk_log2), blk_log2).astype(F32)
        lane8 = lax.broadcasted_iota(jnp.int32, tot.shape, 1)
        incl = padded
        shift = 1
        while shift < N_EXPERTS:
            incl = incl + jnp.where(lane8 >= shift, pltpu.roll(incl, shift, 1), 0.0)
            shift *= 2
        run[...] = incl - padded

    @pl.when(phase == 1)
    def _():
        r = lax.broadcasted_iota(jnp.int32, (tm, tm), 0)
        c = lax.broadcasted_iota(jnp.int32, (tm, tm), 1)
        before = _dot((c < r).astype(BF16), oh.astype(BF16)) + run[0:1, :]
        pos = jnp.concatenate([jnp.sum(o * before, axis=-1, keepdims=True) for o in ohs], axis=-1)
        pos_ref[...] = pos.astype(jnp.int32)
        run[...] += colsum
        cnt_ref[...] = tot[...].astype(jnp.int32)


def _route_positions(top_idx, blk):
    T = top_idx.shape[0]
    tm = min(T, 512)
    assert blk & (blk - 1) == 0 and N_EXPERTS <= LANES
    pos, cnt = pl.pallas_call(
        functools.partial(_route_kernel, blk_log2=blk.bit_length() - 1),
        grid=(2, T // tm),
        in_specs=[pl.BlockSpec((tm, TOP_K), lambda ph, t: (t, 0))],
        out_specs=[pl.BlockSpec((tm, TOP_K), lambda ph, t: (t * ph, 0)),
                   pl.BlockSpec((SUBLANES, LANES), lambda ph, t: (0, 0))],
        out_shape=[jax.ShapeDtypeStruct((T, TOP_K), jnp.int32),
                   jax.ShapeDtypeStruct((SUBLANES, LANES), jnp.int32)],
        scratch_shapes=[pltpu.VMEM((SUBLANES, LANES), F32)] * 2,
        compiler_params=_cparams(("arbitrary", "arbitrary")),
        name="route_positions",
    )(top_idx)
    return pos, cnt[0, :N_EXPERTS]


LOOKAHEAD = 2
N_SLOTS = LOOKAHEAD + 1


def _expert_kernel(be_ref, nu_ref, rt_ref, h_ref, wu_ref, bu_ref, wd_ref, bd_ref, o_ref, xbuf, xb, sem, *, blk):
    j = pl.program_id(0)
    last = nu_ref[0] - 1
    slot = lambda step: step % N_SLOTS

    def row_copy(src_row, s, r):
        return pltpu.make_async_copy(h_ref.at[pl.ds(src_row, 1), :], xbuf.at[s, pl.ds(r, 1), :], sem.at[s])

    def drain(s):
        def body(r, carry):
            row_copy(0, s, r).wait()
            return carry

        lax.fori_loop(0, blk, body, 0, unroll=8)

    @pl.when(j == 0)
    def _():
        for a in range(LOOKAHEAD):
            base = jnp.minimum(a, last) * blk

            def body(r, carry, base=base, a=a):
                row_copy(rt_ref[base + r], a, r).start()
                return carry

            lax.fori_loop(0, blk, body, 0, unroll=8)

    @pl.when(j <= last)
    def _():
        drain(slot(j))
        xb[...] = xbuf[slot(j)].astype(BF16)
        base = jnp.minimum(j + LOOKAHEAD, last) * blk
        for r in range(blk):
            row_copy(rt_ref[base + r], slot(j + LOOKAHEAD), r).start()
        up = _dot(xb[...], wu_ref[...]) + bu_ref[...]
        glu = jnp.minimum(up[:, :D_FF], SWIGLU_LIMIT)
        lin = jnp.clip(up[:, D_FF:], -SWIGLU_LIMIT, SWIGLU_LIMIT)
        act = glu * jax.nn.sigmoid(SWIGLU_ALPHA * glu) * (lin + 1.0)
        o_ref[...] = _dot(act.astype(BF16), wd_ref[...]) + bd_ref[...]

        @pl.when(j == last)
        def _():
            for a in range(1, N_SLOTS):
                drain(slot(j + a))

    @pl.when(j > last)
    def _():
        o_ref[...] = jnp.zeros_like(o_ref)


def _experts(h, row_tok, block_expert, n_used, w_up_bf, b_up, w_down_bf, b_down, layer, blk):
    n_rows = row_tok.shape[0]
    wmap = lambda j, be, nu, rt: (layer, be[j], 0, 0)
    return pl.pallas_call(
        functools.partial(_expert_kernel, blk=blk),
        grid_spec=pltpu.PrefetchScalarGridSpec(
            num_scalar_prefetch=3,
            grid=(n_rows // blk,),
            in_specs=[
                pl.BlockSpec(memory_space=pl.ANY),
                pl.BlockSpec((None, None, D_MODEL, 2 * D_FF), wmap),
                pl.BlockSpec((None, None, 1, 2 * D_FF), wmap),
                pl.BlockSpec((None, None, D_FF, D_MODEL), wmap),
                pl.BlockSpec((None, None, 1, D_MODEL), wmap),
            ],
            out_specs=pl.BlockSpec((blk, D_MODEL), lambda j, be, nu, rt: (j, 0)),
            scratch_shapes=[pltpu.VMEM((N_SLOTS, blk, D_MODEL), F32), pltpu.VMEM((blk, D_MODEL), BF16),
                            pltpu.SemaphoreType.DMA((N_SLOTS,))],
        ),
        out_shape=jax.ShapeDtypeStruct((n_rows, D_MODEL), F32),
        compiler_params=_cparams(("arbitrary",), vmem_mb=48),
        name="experts",
    )(block_expert, n_used, row_tok, h, w_up_bf, b_up, w_down_bf, b_down)


def _final_kernel(pos_ref, h_ref, y_ref, gate_ref, p_ref, wg_ref, wp_ref, lg_ref, lb_ref, o_ref, ob_ref,
                  ybuf, ffn_s, sem, *, alpha, tm):
    j = pl.program_id(0)
    last = pl.num_programs(0) - 1
    slot = lambda step: step % N_SLOTS

    def row_copy(src_row, s, k, r):
        return pltpu.make_async_copy(y_ref.at[pl.ds(src_row, 1), :], ybuf.at[s, k, pl.ds(r, 1), :], sem.at[s])

    def drain(s):
        for k in range(TOP_K):
            def body(r, carry, k=k):
                row_copy(0, s, k, r).wait()
                return carry

            lax.fori_loop(0, tm, body, 0, unroll=8)

    @pl.when(j == 0)
    def _():
        for a in range(LOOKAHEAD):
            base = jnp.minimum(a, last) * (tm * TOP_K)
            for k in range(TOP_K):
                def body(r, carry, base=base, a=a, k=k):
                    row_copy(pos_ref[base + r * TOP_K + k], a, k, r).start()
                    return carry

                lax.fori_loop(0, tm, body, 0, unroll=8)

    drain(slot(j))
    ffn = gate_ref[:, 0:1] * ybuf[slot(j), 0]
    for k in range(1, TOP_K):
        ffn = ffn + gate_ref[:, k:k + 1] * ybuf[slot(j), k]
    ffn_s[...] = ffn
    base = jnp.minimum(j + LOOKAHEAD, last) * (tm * TOP_K)
    for r in range(tm):
        for k in range(TOP_K):
            row_copy(pos_ref[base + r * TOP_K + k], slot(j + LOOKAHEAD), k, r).start()
    h = h_ref[...]
    ple = jax.nn.sigmoid(_dot(h.astype(BF16), wg_ref[...])) * _dot(p_ref[...].astype(BF16), wp_ref[...])
    out = _ln(alpha * h + ffn_s[...] + ple, lg_ref[...], lb_ref[...])
    o_ref[...] = out
    ob_ref[...] = out.astype(BF16)

    @pl.when(j == last)
    def _():
        for a in range(1, N_SLOTS):
            drain(slot(j + a))


def _final(h, y_rows, pos, gate, p, ple_w_gate_bf, ple_w_proj_bf, ln_g, ln_b, layer, alpha):
    T = h.shape[0]
    tm = min(T, 128)
    lay2 = lambda t, ps: (layer, 0, 0)
    tile = lambda t, ps: (t, 0)
    return pl.pallas_call(
        functools.partial(_final_kernel, alpha=alpha, tm=tm),
        grid_spec=pltpu.PrefetchScalarGridSpec(
            num_scalar_prefetch=1,
            grid=(T // tm,),
            in_specs=[
                pl.BlockSpec((tm, D_MODEL), tile),
                pl.BlockSpec(memory_space=pl.ANY),
                pl.BlockSpec((tm, TOP_K), tile),
                pl.BlockSpec((None, tm, PLE_DIM), lambda t, ps: (layer, t, 0)),
                pl.BlockSpec((None, D_MODEL, D_MODEL), lay2),
                pl.BlockSpec((None, PLE_DIM, D_MODEL), lay2),
                pl.BlockSpec((None, 1, D_MODEL), lay2),
                pl.BlockSpec((None, 1, D_MODEL), lay2),
            ],
            out_specs=[pl.BlockSpec((tm, D_MODEL), tile)] * 2,
            scratch_shapes=[pltpu.VMEM((N_SLOTS, TOP_K, tm, D_MODEL), F32), pltpu.VMEM((tm, D_MODEL), F32),
                            pltpu.SemaphoreType.DMA((N_SLOTS,))],
        ),
        out_shape=[jax.ShapeDtypeStruct((T, D_MODEL), F32), jax.ShapeDtypeStruct((T, D_MODEL), BF16)],
        compiler_params=_cparams(("arbitrary",), vmem_mb=48),
        name="combine_ple_ln2",
    )(pos, h, y_rows, gate, p, ple_w_gate_bf, ple_w_proj_bf, ln_g, ln_b)


def _moe_route(top_idx, blk):
    T = top_idx.shape[0]
    n_pairs = T * TOP_K
    pos, counts = _route_positions(top_idx, blk)
    pos = pos.reshape(n_pairs)
    pad_end = jnp.cumsum((counts + blk - 1) // blk * blk)
    n_blocks = -(-n_pairs // blk) + N_EXPERTS
    n_rows = n_blocks * blk
    tok = jnp.arange(n_pairs, dtype=jnp.int32) // TOP_K
    row_tok = (jnp.arange(n_rows, dtype=jnp.int32) % T).at[pos].set(
        tok, unique_indices=True, mode='promise_in_bounds')
    block_start = jnp.arange(n_blocks, dtype=jnp.int32) * blk
    block_expert = jnp.minimum(jnp.sum((pad_end[None, :] <= block_start[:, None]).astype(jnp.int32), axis=1),
                               N_EXPERTS - 1)
    n_used = pad_end[-1:].astype(jnp.int32) // blk
    return row_tok, block_expert, n_used, pos


def _moe_ffn_rows(h, top_idx, w_up_bf, b_up, w_down_bf, b_down, layer, blk):
    row_tok, block_expert, n_used, pos = _moe_route(top_idx, blk)
    return _experts(h, row_tok, block_expert, n_used, w_up_bf, b_up, w_down_bf, b_down, layer, blk), pos


def kernel(x_prompt, x_sample, state_ret, cache_dil_k, cache_dil_v, p_prompt, p_sample, w_in, sgu_ln_g, sgu_ln_b, sgu_w, sgu_b, w_branch, w_out, ln1_g, ln1_b, router_w, router_b, exp_w_up, exp_b_up, exp_w_down, exp_b_down, ple_w_gate, ple_w_proj, ln2_g, ln2_b):
    B, S, D = x_prompt.shape
    Bd, C, _ = x_sample.shape
    depth = w_in.shape[0]
    w_buf = cache_dil_k.shape[2]
    alpha = (2 * depth) ** 0.25
    Tp, Ts = B * S, Bd * C
    H, hd = DIL_HEADS, DIL_HD

    w_in_bf, w_branch_bf, w_out_bf = w_in.astype(BF16), w_branch.astype(BF16), w_out.astype(BF16)
    w_up_bf, w_down_bf = exp_w_up.astype(BF16), exp_w_down.astype(BF16)
    ple_w_gate_bf, ple_w_proj_bf = ple_w_gate.astype(BF16), ple_w_proj.astype(BF16)
    row3 = lambda a: a.reshape(depth, 1, a.shape[-1])
    sgu_ln_g3, sgu_ln_b3 = row3(sgu_ln_g), row3(sgu_ln_b)
    ln1_g3, ln1_b3, ln2_g3, ln2_b3 = row3(ln1_g), row3(ln1_b), row3(ln2_g), row3(ln2_b)
    router_b3 = row3(router_b)
    b_up4 = exp_b_up.reshape(depth, N_EXPERTS, 1, 2 * D_FF)
    b_down4 = exp_b_down.reshape(depth, N_EXPERTS, 1, D_MODEL)
    sgu_b4 = sgu_b.reshape(depth, SGU_GROUPS, SGU_CHUNK, 1)
    p_prompt3 = p_prompt.reshape(depth, Tp, PLE_DIM)
    p_sample3 = p_sample.reshape(depth, Ts, PLE_DIM)

    pos_p = jnp.arange(S, dtype=jnp.int32)
    pos_s = PAST_LEN + jnp.arange(C, dtype=jnp.int32)
    ret_cos_p, ret_sin_p = _rot_tables(pos_p, _retention_inv_freq())
    ret_cos_s, ret_sin_s = _rot_tables(pos_s, _retention_inv_freq())
    dil_cos_p, dil_sin_p = _rot_tables(pos_p, _rope_inv_freq(hd))
    dil_cos_s, dil_sin_s = _rot_tables(pos_s, _rope_inv_freq(hd))
    ret_tabs_p = _retention_tables(RET_CHUNK)
    ret_tabs_s = _retention_tables(C)
    cnt_a, cnt_b, cnt_n, tail, period = _sample_key_tables(C, w_buf)
    key_tables = (jnp.asarray(cnt_a), jnp.asarray(cnt_b), jnp.asarray(cnt_n), tail, period)

    xp, xs = x_prompt.reshape(Tp, D), x_sample.reshape(Ts, D)
    xp_bf, xs_bf = xp.astype(BF16), xs.astype(BF16)
    ret_p, ret_s, k_p, v_p, k_s, v_s, sgu_s = [], [], [], [], [], [], []

    def channel_mix(x, out_a, out_b, out_c, cols, p3, i, blk):
        h, top_idx, gate = _merge(out_a, out_b, out_c, cols, x, w_branch_bf, w_out_bf, ln1_g3, ln1_b3,
                                  router_w, router_b3, i, alpha)
        y_rows, pos = _moe_ffn_rows(h, top_idx, w_up_bf, b_up4, w_down_bf, b_down4, i, blk)
        return _final(h, y_rows, pos, gate, p3, ple_w_gate_bf, ple_w_proj_bf, ln2_g3, ln2_b3, i, alpha)

    heads_last = lambda a: a.reshape(Bd, H, C, hd).transpose(0, 2, 1, 3)

    for i in range(depth):
        cols = _in_proj(xp_bf, w_in_bf, i)
        out_a, st = _retention_prompt(cols, B, S, ret_cos_p, ret_sin_p, ret_tabs_p)
        out_b = _sgu_prompt(cols, Tp, sgu_ln_g3, sgu_ln_b3, sgu_w, sgu_b4, i)
        out_c, kr, vr = _dilated_prompt(cols, B, S, dil_cos_p, dil_sin_p)
        xp, xp_bf = channel_mix(xp, out_a, out_b, out_c, cols, p_prompt3, i, MOE_ROWS_PROMPT)
        ret_p.append(st)
        k_p.append(kr.reshape(B, S, H, hd))
        v_p.append(vr.reshape(B, S, H, hd))
        cols = _in_proj(xs_bf, w_in_bf, i)
        cols3 = cols.reshape(Bd, C, N_IN_COLS)
        out_a, out_b, st, vn = _ret_sgu_sample(cols3, state_ret, ret_cos_s, ret_sin_s, ret_tabs_s,
                                               sgu_ln_g3, sgu_ln_b3, sgu_w, sgu_b4, i)
        out_c, kr, vr = _dilated_sample(cols3, cache_dil_k, cache_dil_v, dil_cos_s, dil_sin_s, key_tables, i)
        xs, xs_bf = channel_mix(xs, out_a.reshape(Ts, D), out_b.reshape(Ts, D), heads_last(out_c).reshape(Ts, D),
                                cols, p_sample3, i, MOE_ROWS_SAMPLE)
        ret_s.append(st)
        k_s.append(heads_last(kr))
        v_s.append(heads_last(vr))
        sgu_s.append(vn)

    return (xp.reshape(B, S, D), xs.reshape(Bd, C, D), jnp.stack(ret_p), jnp.stack(ret_s),
            jnp.stack(k_p), jnp.stack(v_p),
            jnp.stack(k_s), jnp.stack(v_s), jnp.stack(sgu_s))
```

```python
import functools

import numpy as np
import jax
import jax.numpy as jnp
from jax import lax
from jax.experimental import pallas as pl
from jax.experimental.pallas import tpu as pltpu

F32, BF16 = jnp.float32, jnp.bfloat16

D_MODEL = 1024
PAST_LEN = 8192
RET_HEADS, RET_QK, RET_V, RET_CHUNK = 4, 128, 256, 128
SGU_GROUPS, SGU_CHUNK = 4, 128
SGU_GW = D_MODEL // SGU_GROUPS
DIL_HEADS, DIL_HD, DIL_BLOCK = 8, 128, 128
DIL_PATTERNS = ((128, 1), (512, 4), (2048, 16))
ROPE_THETA = 10000.0
N_BRANCHES = 3
N_EXPERTS, TOP_K, D_FF = 32, 4, 1024
SWIGLU_ALPHA, SWIGLU_LIMIT = 1.702, 7.0
PLE_DIM = 256
LN_EPS = 1e-5

C_AQ, C_AK, C_AV, C_AG = 0, 512, 1024, 2048
C_BU, C_BV = 3072, 4096
C_CQ, C_CK, C_CV = 5120, 6144, 7168
C_GATE = 8192
N_IN_COLS = C_GATE + N_BRANCHES * D_MODEL

LANES = 128
SUBLANES = 8
MOE_ROWS_PROMPT = 256
MOE_ROWS_SAMPLE = 128


def _cparams(sem, vmem_mb=None):
    kw = dict(dimension_semantics=sem)
    if vmem_mb is not None:
        kw["vmem_limit_bytes"] = vmem_mb * 2 ** 20
    return pltpu.CompilerParams(**kw)


def _dot(a, b):
    return jnp.dot(a, b, preferred_element_type=F32)


def _dot_nt(a, b):
    return lax.dot_general(a, b, (((1,), (1,)), ((), ())), preferred_element_type=F32)


def _dot_tn(a, b):
    return lax.dot_general(a, b, (((0,), (0,)), ((), ())), preferred_element_type=F32)


def _ln(x, g, b):
    mu = jnp.mean(x, axis=-1, keepdims=True)
    xc = x - mu
    var = jnp.mean(xc * xc, axis=-1, keepdims=True)
    return xc * lax.rsqrt(var + LN_EPS) * g + b


def _rot(x, cosf, sinf):
    return x * cosf + pltpu.roll(x, LANES // 2, 1) * sinf


def _rot_tables(pos, inv_freq):
    ang = pos.astype(F32)[:, None] * inv_freq[None, :]
    cos, sin = jnp.cos(ang), jnp.sin(ang)
    return jnp.concatenate([cos, cos], axis=-1), jnp.concatenate([-sin, sin], axis=-1)


def _retention_inv_freq():
    return 1.0 / (10000.0 ** jnp.linspace(0.0, 1.0, RET_QK // 2, dtype=F32))


def _rope_inv_freq(hd):
    return ROPE_THETA ** (-jnp.arange(0, hd, 2, dtype=F32) / hd)


def _retention_tables(C):
    log_g = jnp.log1p(-jnp.exp2(-5.0 - jnp.arange(RET_HEADS, dtype=F32)))
    n = jnp.arange(C, dtype=F32)
    diff = n[:, None] - n[None, :]
    decay = jnp.where(diff >= 0, jnp.exp(log_g[:, None, None] * jnp.maximum(diff, 0.0)), 0.0)
    qscale = jnp.exp(log_g[:, None] * (n[None, :] + 1.0))[:, :, None]
    kscale = jnp.exp(log_g[:, None] * (C - 1.0 - n[None, :]))[:, :, None]
    gc = jnp.exp(log_g * C)[:, None, None]
    return decay, qscale, kscale, gc


def _mm_kernel(x_ref, w_ref, o_ref):
    o_ref[...] = _dot(x_ref[...], w_ref[...]).astype(o_ref.dtype)


def _in_proj(x_bf, w_in_bf, layer):
    T = x_bf.shape[0]
    tm = min(T, 2048)
    tn = 1024
    return pl.pallas_call(
        _mm_kernel,
        grid=(N_IN_COLS // tn, T // tm),
        in_specs=[pl.BlockSpec((tm, D_MODEL), lambda n, t: (t, 0)),
                  pl.BlockSpec((None, D_MODEL, tn), lambda n, t: (layer, 0, n))],
        out_specs=pl.BlockSpec((tm, tn), lambda n, t: (t, n)),
        out_shape=jax.ShapeDtypeStruct((T, N_IN_COLS), BF16),
        compiler_params=_cparams(("parallel", "arbitrary")),
        name="in_proj",
    )(x_bf, w_in_bf)


def _group_norm_gate(o, g):
    mu = jnp.mean(o, axis=-1, keepdims=True)
    oc = o - mu
    var = jnp.mean(oc * oc, axis=-1, keepdims=True)
    return oc * lax.rsqrt(var + LN_EPS) * jax.nn.silu(g)


def _ret_prompt_kernel(q_ref, k_ref, v_ref, g_ref, cos_ref, sin_ref, decay_ref, qs_ref, ks_ref, gc_ref,
                       o_ref, st_ref, state, *, n_chunks):
    c = pl.program_id(1)

    @pl.when(c == 0)
    def _():
        state[...] = jnp.zeros_like(state)

    cosf, sinf = cos_ref[...], sin_ref[...]
    for h in range(RET_HEADS):
        qk = slice(h * RET_QK, (h + 1) * RET_QK)
        vv = slice(h * RET_V, (h + 1) * RET_V)
        q = _rot(q_ref[:, qk].astype(F32), cosf, sinf)
        k = _rot(k_ref[:, qk].astype(F32), cosf, sinf) * RET_QK ** -0.5
        v = v_ref[:, vv]
        qb = q.astype(BF16)
        s = _dot_nt(qb, k.astype(BF16)) * decay_ref[h]
        o = _dot(s.astype(BF16), v)
        st = state[h]
        o = o + _dot(qb, st.astype(BF16)) * qs_ref[h]
        kw = (k * ks_ref[h]).astype(BF16)
        state[h] = st * gc_ref[h] + _dot_tn(kw, v)
        o_ref[:, vv] = _group_norm_gate(o, g_ref[:, vv].astype(F32)).astype(o_ref.dtype)

    @pl.when(c == n_chunks - 1)
    def _():
        st_ref[...] = state[...]


def _retention_prompt(cols, B, S, cosf, sinf, tabs):
    C = RET_CHUNK
    nc = S // C
    decay, qscale, kscale, gc = tabs
    H = RET_HEADS
    row = lambda b, c: b * nc + c
    return pl.pallas_call(
        functools.partial(_ret_prompt_kernel, n_chunks=nc),
        grid=(B, nc),
        in_specs=[
            pl.BlockSpec((C, H * RET_QK), lambda b, c: (row(b, c), C_AQ // (H * RET_QK))),
            pl.BlockSpec((C, H * RET_QK), lambda b, c: (row(b, c), C_AK // (H * RET_QK))),
            pl.BlockSpec((C, H * RET_V), lambda b, c: (row(b, c), C_AV // (H * RET_V))),
            pl.BlockSpec((C, H * RET_V), lambda b, c: (row(b, c), C_AG // (H * RET_V))),
            pl.BlockSpec((C, RET_QK), lambda b, c: (c, 0)),
            pl.BlockSpec((C, RET_QK), lambda b, c: (c, 0)),
            pl.BlockSpec((H, C, C), lambda b, c: (0, 0, 0)),
            pl.BlockSpec((H, C, 1), lambda b, c: (0, 0, 0)),
            pl.BlockSpec((H, C, 1), lambda b, c: (0, 0, 0)),
            pl.BlockSpec((H, 1, 1), lambda b, c: (0, 0, 0)),
        ],
        out_specs=[
            pl.BlockSpec((C, H * RET_V), lambda b, c: (row(b, c), 0)),
            pl.BlockSpec((None, H, RET_QK, RET_V), lambda b, c: (b, 0, 0, 0)),
        ],
        out_shape=[jax.ShapeDtypeStruct((B * S, H * RET_V), BF16),
                   jax.ShapeDtypeStruct((B, H, RET_QK, RET_V), F32)],
        scratch_shapes=[pltpu.VMEM((H, RET_QK, RET_V), F32)],
        compiler_params=_cparams(("parallel", "arbitrary")),
        name="retention_prompt",
    )(cols, cols, cols, cols, cosf, sinf, decay, qscale, kscale, gc)


def _sgu_weights(w_ref):
    r = lax.broadcasted_iota(jnp.int32, (SGU_CHUNK, SGU_CHUNK), 0)
    c = lax.broadcasted_iota(jnp.int32, (SGU_CHUNK, SGU_CHUNK), 1)
    return [jnp.where(r >= c, w_ref[g], 0.0) for g in range(SGU_GROUPS)]


def _sgu_prompt_kernel(u_ref, v_ref, lg_ref, lb_ref, w_ref, bs_ref, o_ref, *, n_chunks):
    w = [x.astype(BF16) for x in _sgu_weights(w_ref)]
    for c in range(n_chunks):
        rows = slice(c * SGU_CHUNK, (c + 1) * SGU_CHUNK)
        u = jax.nn.gelu(u_ref[rows, :].astype(F32))
        vn = _ln(jax.nn.gelu(v_ref[rows, :].astype(F32)), lg_ref[...], lb_ref[...])
        for g in range(SGU_GROUPS):
            cs = slice(g * SGU_GW, (g + 1) * SGU_GW)
            f = _dot(w[g], vn[:, cs].astype(BF16)) + bs_ref[g]
            o_ref[rows, cs] = (u[:, cs] * f).astype(o_ref.dtype)


def _sgu_prompt(cols, T, sgu_ln_g, sgu_ln_b, sgu_w, sgu_b4, layer):
    n_chunks = 4
    R = n_chunks * SGU_CHUNK
    return pl.pallas_call(
        functools.partial(_sgu_prompt_kernel, n_chunks=n_chunks),
        grid=(T // R,),
        in_specs=[
            pl.BlockSpec((R, D_MODEL), lambda t: (t, C_BU // D_MODEL)),
            pl.BlockSpec((R, D_MODEL), lambda t: (t, C_BV // D_MODEL)),
            pl.BlockSpec((None, 1, D_MODEL), lambda t: (layer, 0, 0)),
            pl.BlockSpec((None, 1, D_MODEL), lambda t: (layer, 0, 0)),
            pl.BlockSpec((None, SGU_GROUPS, SGU_CHUNK, SGU_CHUNK), lambda t: (layer, 0, 0, 0)),
            pl.BlockSpec((None, SGU_GROUPS, SGU_CHUNK, 1), lambda t: (layer, 0, 0, 0)),
        ],
        out_specs=pl.BlockSpec((R, D_MODEL), lambda t: (t, 0)),
        out_shape=jax.ShapeDtypeStruct((T, D_MODEL), BF16),
        compiler_params=_cparams(("parallel",)),
        name="sgu_prompt",
    )(cols, cols, sgu_ln_g, sgu_ln_b, sgu_w, sgu_b4)


def _dil_prompt_kernel(q_ref, k_ref, v_ref, cos_ref, sin_ref, o_ref, ko_ref, vo_ref, qp, kp, vp, o_p, lse_p, *, S):
    BLK, hd = DIL_BLOCK, DIL_HD
    nblk = S // BLK
    scale = hd ** -0.5
    RC = 256

    def rot_body(j, carry):
        r = pl.ds(pl.multiple_of(j * RC, RC), RC)
        cosf, sinf = cos_ref[r, :], sin_ref[r, :]
        k = _rot(k_ref[r, :].astype(F32), cosf, sinf)
        v = v_ref[r, :].astype(F32)
        qp[r, :] = _rot(q_ref[r, :].astype(F32), cosf, sinf)
        kp[r, :] = k
        vp[r, :] = v
        ko_ref[r, :] = k
        vo_ref[r, :] = v
        return carry

    lax.fori_loop(0, S // RC, rot_body, 0)

    qi = lax.broadcasted_iota(jnp.int32, (BLK, BLK), 0)
    kj = lax.broadcasted_iota(jnp.int32, (BLK, BLK), 1)
    lower = kj <= qi
    upper = kj >= qi
    ones_v = jnp.ones((BLK, hd), BF16)

    def block(p, d, j):
        nps = nblk // d
        if d == 1:
            n, start = j, pl.multiple_of(j * BLK, BLK)
            rows = lambda st: pl.ds(pl.multiple_of(st, BLK), BLK)
        else:
            n = j % nps
            start = j // nps + n * (BLK * d)
            rows = lambda st: pl.ds(st, BLK, stride=d)
        v_aug = lambda st: jnp.concatenate([vp[rows(st), :].astype(BF16), ones_v], axis=1)
        qb = qp[rows(start), :].astype(BF16)
        s_c = jnp.where(lower, _dot_nt(qb, kp[rows(start), :].astype(BF16)) * scale, -jnp.inf)
        m = jnp.max(s_c, axis=-1, keepdims=True)
        if nps > 1:
            prev = jnp.where(n > 0, start - BLK * d, start)
            no_prev = jnp.where(n > 0, 0.0, -jnp.inf)
            s_p = jnp.where(upper, _dot_nt(qb, kp[rows(prev), :].astype(BF16)) * scale + no_prev, -jnp.inf)
            m = jnp.maximum(m, jnp.max(s_p, axis=-1, keepdims=True))
        acc = _dot(jnp.exp(s_c - m).astype(BF16), v_aug(start))
        if nps > 1:
            acc = acc + _dot(jnp.exp(s_p - m).astype(BF16), v_aug(prev))
        den = acc[:, hd:]
        o_p[p, rows(start), :] = acc[:, :hd] / den
        lse_p[p, rows(start), :] = m + jnp.log(den)

    def blocks(j, carry):
        for p, (_, d) in enumerate(DIL_PATTERNS):
            block(p, d, j)
        return carry

    lax.fori_loop(0, nblk, blocks, 0, unroll=4)

    def combine(j, carry):
        r = pl.ds(pl.multiple_of(j * BLK, BLK), BLK)
        lses = [lse_p[p, r, :] for p in range(len(DIL_PATTERNS))]
        m = functools.reduce(jnp.maximum, lses)
        ws = [jnp.exp(l - m) for l in lses]
        num = sum(w * o_p[p, r, :] for p, w in enumerate(ws))
        o_ref[r, :] = (num / sum(ws)).astype(o_ref.dtype)
        return carry

    lax.fori_loop(0, nblk, combine, 0)


def _dilated_prompt(cols, B, S, cosf, sinf):
    H, hd = DIL_HEADS, DIL_HD
    P = len(DIL_PATTERNS)
    assert all(w // d == DIL_BLOCK for w, d in DIL_PATTERNS) and DIL_PATTERNS[0][1] == 1
    assert all((S // DIL_BLOCK) % d == 0 for _, d in DIL_PATTERNS)
    col = lambda off: (lambda b, h: (b, off // hd + h))
    vm = lambda shape, dt: pltpu.VMEM(shape, dt)
    return pl.pallas_call(
        functools.partial(_dil_prompt_kernel, S=S),
        grid=(B, H),
        in_specs=[
            pl.BlockSpec((S, hd), col(C_CQ)),
            pl.BlockSpec((S, hd), col(C_CK)),
            pl.BlockSpec((S, hd), col(C_CV)),
            pl.BlockSpec((S, hd), lambda b, h: (0, 0)),
            pl.BlockSpec((S, hd), lambda b, h: (0, 0)),
        ],
        out_specs=[pl.BlockSpec((S, hd), lambda b, h: (b, h))] * 3,
        out_shape=[jax.ShapeDtypeStruct((B * S, H * hd), BF16),
                   jax.ShapeDtypeStruct((B * S, H * hd), F32),
                   jax.ShapeDtypeStruct((B * S, H * hd), F32)],
        scratch_shapes=[vm((S, hd), F32)] * 3 + [vm((P, S, hd), F32), vm((P, S, hd), F32)],
        compiler_params=_cparams(("parallel", "arbitrary"), vmem_mb=40),
        name="dilated_prompt",
    )(cols, cols, cols, cosf, sinf)


def _ret_sgu_sample_kernel(cols_ref, st_ref, cos_ref, sin_ref, decay_ref, qs_ref, ks_ref, gc_ref,
                           lg_ref, lb_ref, w_ref, bs_ref, oa_ref, ob_ref, sto_ref, vn_ref, *, C):
    cosf, sinf = cos_ref[...], sin_ref[...]
    for h in range(RET_HEADS):
        q = _rot(cols_ref[:, C_AQ + h * RET_QK:C_AQ + (h + 1) * RET_QK].astype(F32), cosf, sinf)
        k = _rot(cols_ref[:, C_AK + h * RET_QK:C_AK + (h + 1) * RET_QK].astype(F32), cosf, sinf) * RET_QK ** -0.5
        v = cols_ref[:, C_AV + h * RET_V:C_AV + (h + 1) * RET_V].astype(F32)
        g = cols_ref[:, C_AG + h * RET_V:C_AG + (h + 1) * RET_V].astype(F32)
        s = _dot_nt(q.astype(BF16), k.astype(BF16)) * decay_ref[h]
        o = s[:, 0:1] * v[0:1, :]
        for j in range(1, C):
            o = o + s[:, j:j + 1] * v[j:j + 1, :]
        st = st_ref[h]
        o = o + _dot(q.astype(BF16), st.astype(BF16)) * qs_ref[h]
        kw = (k * ks_ref[h]).astype(BF16)
        sto_ref[h] = st * gc_ref[h] + _dot_tn(kw, v.astype(BF16))
        oa_ref[:, h * RET_V:(h + 1) * RET_V] = _group_norm_gate(o, g).astype(oa_ref.dtype)
    u = jax.nn.gelu(cols_ref[:, C_BU:C_BU + D_MODEL].astype(F32))
    vn = _ln(jax.nn.gelu(cols_ref[:, C_BV:C_BV + D_MODEL].astype(F32)), lg_ref[...], lb_ref[...])
    vn_ref[...] = vn
    w = _sgu_weights(w_ref)
    for g in range(SGU_GROUPS):
        cs = slice(g * SGU_GW, (g + 1) * SGU_GW)
        wg = w[g][:C, :C]
        f = bs_ref[g][:C, :]
        for j in range(C):
            f = f + wg[:, j:j + 1] * vn[j:j + 1, cs]
        ob_ref[:, cs] = (u[:, cs] * f).astype(ob_ref.dtype)


def _ret_sgu_sample(cols3, state_ret, cosf, sinf, tabs, sgu_ln_g, sgu_ln_b, sgu_w, sgu_b4, layer):
    Bd, C, _ = cols3.shape
    decay, qscale, kscale, gc = tabs
    H = RET_HEADS
    const3 = lambda b: (0, 0, 0)
    return pl.pallas_call(
        functools.partial(_ret_sgu_sample_kernel, C=C),
        grid=(Bd,),
        in_specs=[
            pl.BlockSpec((None, C, N_IN_COLS), lambda b: (b, 0, 0)),
            pl.BlockSpec((None, None, H, RET_QK, RET_V), lambda b: (layer, b, 0, 0, 0)),
            pl.BlockSpec((C, RET_QK), lambda b: (0, 0)),
            pl.BlockSpec((C, RET_QK), lambda b: (0, 0)),
            pl.BlockSpec((H, C, C), const3),
            pl.BlockSpec((H, C, 1), const3),
            pl.BlockSpec((H, C, 1), const3),
            pl.BlockSpec((H, 1, 1), const3),
            pl.BlockSpec((None, 1, D_MODEL), lambda b: (layer, 0, 0)),
            pl.BlockSpec((None, 1, D_MODEL), lambda b: (layer, 0, 0)),
            pl.BlockSpec((None, SGU_GROUPS, SGU_CHUNK, SGU_CHUNK), lambda b: (layer, 0, 0, 0)),
            pl.BlockSpec((None, SGU_GROUPS, SGU_CHUNK, 1), lambda b: (layer, 0, 0, 0)),
        ],
        out_specs=[
            pl.BlockSpec((None, C, D_MODEL), lambda b: (b, 0, 0)),
            pl.BlockSpec((None, C, D_MODEL), lambda b: (b, 0, 0)),
            pl.BlockSpec((None, H, RET_QK, RET_V), lambda b: (b, 0, 0, 0)),
            pl.BlockSpec((None, C, D_MODEL), lambda b: (b, 0, 0)),
        ],
        out_shape=[jax.ShapeDtypeStruct((Bd, C, D_MODEL), BF16),
                   jax.ShapeDtypeStruct((Bd, C, D_MODEL), BF16),
                   jax.ShapeDtypeStruct((Bd, H, RET_QK, RET_V), F32),
                   jax.ShapeDtypeStruct((Bd, C, D_MODEL), F32)],
        compiler_params=_cparams(("parallel",)),
        name="ret_sgu_sample",
    )(cols3, state_ret, cosf, sinf, decay, qscale, kscale, gc, sgu_ln_g, sgu_ln_b, sgu_w, sgu_b4)


def _sample_key_tables(C, w_buf):
    H = DIL_HEADS
    per_pattern = []
    for window, dil in DIL_PATTERNS:
        past, new = np.zeros((C, w_buf), np.float32), np.zeros((C, C), np.float32)
        for t in range(C):
            for j in range(window // dil + 1):
                idx = w_buf + t - dil * j
                assert idx >= 0
                if idx < w_buf:
                    past[t, idx] += 1
                else:
                    new[t, idx - w_buf] += 1
        per_pattern.append((past, new))
    tail = max(w for w, _ in DIL_PATTERNS[:-1])
    period = DIL_PATTERNS[-1][1]
    past_tail = sum(p for p, _ in per_pattern[:-1])
    past_wide = per_pattern[-1][0]
    in_wide = (np.arange(w_buf) % period) < C
    assert w_buf % tail == 0 and w_buf % period == 0
    assert not past_tail[:, :w_buf - tail].any() and not past_wide[:, ~in_wide].any()
    new = sum(n for _, n in per_pattern)

    def expand(cnt):
        out = np.zeros((H, C, cnt.shape[1], H), np.float32)
        for h in range(H):
            out[h, :, :, h] = cnt
        return out.reshape(H * C, cnt.shape[1] * H)

    cnt_n = np.zeros((H, C, H, C), np.float32)
    for h in range(H):
        cnt_n[h, :, h, :] = new
    return expand(past_tail[:, w_buf - tail:]), expand(past_wide[:, in_wide]), cnt_n.reshape(H * C, H * C), tail, period


def _dil_sample_kernel(q_ref, k_ref, v_ref, ka_ref, va_ref, kb_ref, vb_ref, cos_ref, sin_ref,
                       ca_ref, cb_ref, cn_ref, o_ref, ko_ref, vo_ref, s_scr, *, C):
    H, hd = DIL_HEADS, DIL_HD
    scale = hd ** -0.5
    KC = 1024
    cosf, sinf = cos_ref[...], sin_ref[...]
    heads = [slice(h * hd, (h + 1) * hd) for h in range(H)]
    q_all, k_all, v_all = q_ref[...].astype(F32), k_ref[...].astype(F32), v_ref[...].astype(F32)
    q_rows = jnp.concatenate([_rot(q_all[:, cs], cosf, sinf) for cs in heads], axis=0)
    k_new = jnp.concatenate([_rot(k_all[:, cs], cosf, sinf) for cs in heads], axis=0)
    v_new = jnp.concatenate([v_all[:, cs] for cs in heads], axis=0)
    ko_ref[...] = k_new
    vo_ref[...] = v_new
    qb = q_rows.astype(BF16)

    na, nb = ka_ref.shape[0] * H // KC, kb_ref.shape[0] * C * H // KC
    ra, rb = KC // H, KC // (C * H)
    chunks = [(ka_ref, va_ref, ca_ref, c, slice(c * ra, (c + 1) * ra)) for c in range(na)]
    chunks += [(kb_ref, vb_ref, cb_ref, c, slice(c * rb, (c + 1) * rb)) for c in range(nb)]

    cnt_n = cn_ref[...]
    s_n = jnp.where(cnt_n > 0, _dot_nt(qb, k_new.astype(BF16)) * scale, -jnp.inf)
    m = jnp.max(s_n, axis=-1, keepdims=True)
    for i, (kref, _, cref, c, rows) in enumerate(chunks):
        cols = slice(c * KC, (c + 1) * KC)
        s = _dot_nt(qb, kref[rows].reshape(KC, hd).astype(BF16)) * scale
        s = jnp.where(cref[:, cols] > 0, s, -jnp.inf)
        s_scr[:, i * KC:(i + 1) * KC] = s
        m = jnp.maximum(m, jnp.max(s, axis=-1, keepdims=True))
    e_n = cnt_n * jnp.exp(s_n - m)
    l = jnp.sum(e_n, axis=-1, keepdims=True)
    acc = _dot(e_n.astype(BF16), v_new.astype(BF16))
    for i, (_, vref, cref, c, rows) in enumerate(chunks):
        cols = slice(c * KC, (c + 1) * KC)
        e = cref[:, cols] * jnp.exp(s_scr[:, i * KC:(i + 1) * KC] - m)
        l = l + jnp.sum(e, axis=-1, keepdims=True)
        acc = acc + _dot(e.astype(BF16), vref[rows].reshape(KC, hd).astype(BF16))
    o_ref[...] = (acc / l).astype(o_ref.dtype)


def _dilated_sample(cols3, cache_k, cache_v, cosf, sinf, tables, layer):
    Bd, C, _ = cols3.shape
    depth, _, w_buf, H, hd = cache_k.shape
    cnt_a, cnt_b, cnt_n, tail, period = tables
    W = H * hd
    tail_view = lambda a: a.reshape(depth, Bd, w_buf // tail, tail, H, hd)
    wide_view = lambda a: a.reshape(depth, Bd, w_buf // period, period, H, hd)
    colblk = lambda off: (lambda b: (b, 0, off // W))
    const2 = lambda b: (0, 0)
    tail_spec = pl.BlockSpec((None, None, None, tail, H, hd), lambda b: (layer, b, w_buf // tail - 1, 0, 0, 0))
    wide_spec = pl.BlockSpec((None, None, w_buf // period, C, H, hd), lambda b: (layer, b, 0, 0, 0, 0))
    n_keys = (tail + w_buf // period * C) * H
    return pl.pallas_call(
        functools.partial(_dil_sample_kernel, C=C),
        grid=(Bd,),
        in_specs=[
            pl.BlockSpec((None, C, W), colblk(C_CQ)),
            pl.BlockSpec((None, C, W), colblk(C_CK)),
            pl.BlockSpec((None, C, W), colblk(C_CV)),
            tail_spec, tail_spec, wide_spec, wide_spec,
            pl.BlockSpec((C, hd), const2),
            pl.BlockSpec((C, hd), const2),
            pl.BlockSpec(cnt_a.shape, const2),
            pl.BlockSpec(cnt_b.shape, const2),
            pl.BlockSpec(cnt_n.shape, const2),
        ],
        out_specs=[pl.BlockSpec((None, H * C, hd), lambda b: (b, 0, 0))] * 3,
        out_shape=[jax.ShapeDtypeStruct((Bd, H * C, hd), BF16),
                   jax.ShapeDtypeStruct((Bd, H * C, hd), F32),
                   jax.ShapeDtypeStruct((Bd, H * C, hd), F32)],
        scratch_shapes=[pltpu.VMEM((H * C, n_keys), F32)],
        compiler_params=_cparams(("parallel",), vmem_mb=40),
        name="dilated_sample",
    )(cols3, cols3, cols3, tail_view(cache_k), tail_view(cache_v), wide_view(cache_k), wide_view(cache_v),
      cosf, sinf, cnt_a, cnt_b, cnt_n)


def _merge_kernel(a_ref, b_ref, c_ref, g0_ref, g1_ref, g2_ref, x_ref, wb_ref, wo_ref, lg_ref, lb_ref,
                  rw_ref, rb_ref, h_ref, idx_ref, gate_ref, *, alpha):
    merged = None
    for n, (br, gt) in enumerate(((a_ref, g0_ref), (b_ref, g1_ref), (c_ref, g2_ref))):
        term = jax.nn.sigmoid(gt[...].astype(F32)) * _dot(br[...], wb_ref[n])
        merged = term if merged is None else merged + term
    y = _dot(merged.astype(BF16), wo_ref[...])
    h = _ln(alpha * x_ref[...] + y, lg_ref[...], lb_ref[...])
    h_ref[...] = h
    h_hi = h.astype(BF16)
    h_lo = (h - h_hi.astype(F32)).astype(BF16)
    rw = rw_ref[...]
    rw_hi = rw.astype(BF16)
    rw_lo = (rw - rw_hi.astype(F32)).astype(BF16)
    logits = _dot(h_hi, rw_hi) + _dot(h_lo, rw_hi) + _dot(h_hi, rw_lo) + rb_ref[...]
    lane = lax.broadcasted_iota(jnp.int32, logits.shape, 1)
    vals, idxs = [], []
    for _ in range(TOP_K):
        mx = jnp.max(logits, axis=-1, keepdims=True)
        am = jnp.min(jnp.where(logits == mx, lane, N_EXPERTS), axis=-1, keepdims=True)
        vals.append(mx)
        idxs.append(am)
        logits = jnp.where(lane == am, -jnp.inf, logits)
    e = jnp.exp(jnp.concatenate(vals, axis=-1) - vals[0])
    idx_ref[...] = jnp.concatenate(idxs, axis=-1)
    gate_ref[...] = e / jnp.sum(e, axis=-1, keepdims=True)


def _merge(out_a, out_b, out_c, cols, x, w_branch_bf, w_out_bf, ln_g, ln_b, router_w, router_b, layer, alpha):
    T = x.shape[0]
    tm = min(T, 512)
    tile = lambda j: (lambda t: (t, j))
    lay2 = lambda t: (layer, 0, 0)
    return pl.pallas_call(
        functools.partial(_merge_kernel, alpha=alpha),
        grid=(T // tm,),
        in_specs=[
            pl.BlockSpec((tm, D_MODEL), tile(0)),
            pl.BlockSpec((tm, D_MODEL), tile(0)),
            pl.BlockSpec((tm, D_MODEL), tile(0)),
            pl.BlockSpec((tm, D_MODEL), tile(C_GATE // D_MODEL)),
            pl.BlockSpec((tm, D_MODEL), tile(C_GATE // D_MODEL + 1)),
            pl.BlockSpec((tm, D_MODEL), tile(C_GATE // D_MODEL + 2)),
            pl.BlockSpec((tm, D_MODEL), tile(0)),
            pl.BlockSpec((None, N_BRANCHES, D_MODEL, D_MODEL), lambda t: (layer, 0, 0, 0)),
            pl.BlockSpec((None, D_MODEL, D_MODEL), lay2),
            pl.BlockSpec((None, 1, D_MODEL), lay2),
            pl.BlockSpec((None, 1, D_MODEL), lay2),
            pl.BlockSpec((None, D_MODEL, N_EXPERTS), lay2),
            pl.BlockSpec((None, 1, N_EXPERTS), lay2),
        ],
        out_specs=[pl.BlockSpec((tm, D_MODEL), tile(0)), pl.BlockSpec((tm, TOP_K), tile(0)),
                   pl.BlockSpec((tm, TOP_K), tile(0))],
        out_shape=[jax.ShapeDtypeStruct((T, D_MODEL), F32), jax.ShapeDtypeStruct((T, TOP_K), jnp.int32),
                   jax.ShapeDtypeStruct((T, TOP_K), F32)],
        compiler_params=_cparams(("parallel",), vmem_mb=48),
        name="merge_ln1_router",
    )(out_a, out_b, out_c, cols, cols, cols, x, w_branch_bf, w_out_bf, ln_g, ln_b, router_w, router_b)


def _route_kernel(idx_ref, pos_ref, cnt_ref, tot, run, *, blk_log2):
    phase, t = pl.program_id(0), pl.program_id(1)
    tm = idx_ref.shape[0]
    idx = idx_ref[...]
    lane = lax.broadcasted_iota(jnp.int32, (tm, LANES), 1)
    ohs = [(idx[:, k:k + 1] == lane).astype(F32) for k in range(TOP_K)]
    oh = functools.reduce(lambda a, b: a + b, ohs)
    colsum = jnp.sum(oh, axis=0, keepdims=True)

    @pl.when((phase == 0) & (t == 0))
    def _():
        tot[...] = jnp.zeros_like(tot)

    @pl.when(phase == 0)
    def _():
        tot[...] += colsum

    @pl.when((phase == 1) & (t == 0))
    def _():
        counts = tot[...].astype(jnp.int32)
        blk = 1 << blk_log2
        padded = lax.shift_left(lax.shift_right_logical(counts + (blk - 1), blk_log2), blk_log2).astype(F32)
        lane8 = lax.broadcasted_iota(jnp.int32, tot.shape, 1)
        incl = padded
        shift = 1
        while shift < N_EXPERTS:
            incl = incl + jnp.where(lane8 >= shift, pltpu.roll(incl, shift, 1), 0.0)
            shift *= 2
        run[...] = incl - padded

    @pl.when(phase == 1)
    def _():
        r = lax.broadcasted_iota(jnp.int32, (tm, tm), 0)
        c = lax.broadcasted_iota(jnp.int32, (tm, tm), 1)
        before = _dot((c < r).astype(BF16), oh.astype(BF16)) + run[0:1, :]
        pos = jnp.concatenate([jnp.sum(o * before, axis=-1, keepdims=True) for o in ohs], axis=-1)
        pos_ref[...] = pos.astype(jnp.int32)
        run[...] += colsum
        cnt_ref[...] = tot[...].astype(jnp.int32)


def _route_positions(top_idx, blk):
    T = top_idx.shape[0]
    tm = min(T, 512)
    assert blk & (blk - 1) == 0 and N_EXPERTS <= LANES
    pos, cnt = pl.pallas_call(
        functools.partial(_route_kernel, blk_log2=blk.bit_length() - 1),
        grid=(2, T // tm),
        in_specs=[pl.BlockSpec((tm, TOP_K), lambda ph, t: (t, 0))],
        out_specs=[pl.BlockSpec((tm, TOP_K), lambda ph, t: (t * ph, 0)),
                   pl.BlockSpec((SUBLANES, LANES), lambda ph, t: (0, 0))],
        out_shape=[jax.ShapeDtypeStruct((T, TOP_K), jnp.int32),
                   jax.ShapeDtypeStruct((SUBLANES, LANES), jnp.int32)],
        scratch_shapes=[pltpu.VMEM((SUBLANES, LANES), F32)] * 2,
        compiler_params=_cparams(("arbitrary", "arbitrary")),
        name="route_positions",
    )(top_idx)
    return pos, cnt[0, :N_EXPERTS]


LOOKAHEAD = 2
N_SLOTS = LOOKAHEAD + 1


def _expert_kernel(be_ref, nu_ref, rt_ref, h_ref, wu_ref, bu_ref, wd_ref, bd_ref, o_ref, *scratch, blk):
    xbufs, sem = scratch[:N_SLOTS], scratch[N_SLOTS]
    j = pl.program_id(0)
    last = nu_ref[0] - 1

    def row_copy(src_row, s, r):
        return pltpu.make_async_copy(h_ref.at[pl.ds(src_row, 1), :], xbufs[s].at[pl.ds(r, 1), :], sem.at[s])

    def drain(s):
        def body(r, carry):
            row_copy(0, s, r).wait()
            return carry

        lax.fori_loop(0, blk, body, 0, unroll=8)

    @pl.when(j == 0)
    def _():
        for a in range(LOOKAHEAD):
            base = jnp.minimum(a, last) * blk

            def body(r, carry, base=base, a=a):
                row_copy(rt_ref[base + r], a, r).start()
                return carry

            lax.fori_loop(0, blk, body, 0, unroll=8)

    def live_step(s):
        drain(s)
        base = jnp.minimum(j + LOOKAHEAD, last) * blk
        for r in range(blk):
            row_copy(rt_ref[base + r], (s + LOOKAHEAD) % N_SLOTS, r).start()
        up = _dot(xbufs[s][...].astype(BF16), wu_ref[...]) + bu_ref[...]
        glu = jnp.minimum(up[:, :D_FF], SWIGLU_LIMIT)
        lin = jnp.clip(up[:, D_FF:], -SWIGLU_LIMIT, SWIGLU_LIMIT)
        act = glu * jax.nn.sigmoid(SWIGLU_ALPHA * glu) * (lin + 1.0)
        o_ref[...] = _dot(act.astype(BF16), wd_ref[...]) + bd_ref[...]

        @pl.when(j == last)
        def _():
            for a in range(1, N_SLOTS):
                drain((s + a) % N_SLOTS)

    for s in range(N_SLOTS):
        pl.when((j <= last) & (j % N_SLOTS == s))(functools.partial(live_step, s))

    @pl.when(j > last)
    def _():
        o_ref[...] = jnp.zeros_like(o_ref)


def _experts(h, row_tok, block_expert, n_used, w_up_bf, b_up, w_down_bf, b_down, layer, blk):
    n_rows = row_tok.shape[0]
    wmap = lambda j, be, nu, rt: (layer, be[j], 0, 0)
    return pl.pallas_call(
        functools.partial(_expert_kernel, blk=blk),
        grid_spec=pltpu.PrefetchScalarGridSpec(
            num_scalar_prefetch=3,
            grid=(n_rows // blk,),
            in_specs=[
                pl.BlockSpec(memory_space=pl.ANY),
                pl.BlockSpec((None, None, D_MODEL, 2 * D_FF), wmap),
                pl.BlockSpec((None, None, 1, 2 * D_FF), wmap),
                pl.BlockSpec((None, None, D_FF, D_MODEL), wmap),
                pl.BlockSpec((None, None, 1, D_MODEL), wmap),
            ],
            out_specs=pl.BlockSpec((blk, D_MODEL), lambda j, be, nu, rt: (j, 0)),
            scratch_shapes=[pltpu.VMEM((blk, D_MODEL), F32)] * N_SLOTS + [pltpu.SemaphoreType.DMA((N_SLOTS,))],
        ),
        out_shape=jax.ShapeDtypeStruct((n_rows, D_MODEL), F32),
        compiler_params=_cparams(("arbitrary",), vmem_mb=48),
        name="experts",
    )(block_expert, n_used, row_tok, h, w_up_bf, b_up, w_down_bf, b_down)


def _final_kernel(pos_ref, h_ref, y_ref, gate_ref, p_ref, wg_ref, wp_ref, lg_ref, lb_ref, o_ref, ob_ref,
                  *scratch, alpha, tm):
    ybufs, sem = scratch[:N_SLOTS], scratch[N_SLOTS]
    j = pl.program_id(0)
    last = pl.num_programs(0) - 1

    def row_copy(src_row, s, k, r):
        return pltpu.make_async_copy(y_ref.at[pl.ds(src_row, 1), :], ybufs[s].at[k, pl.ds(r, 1), :], sem.at[s])

    def drain(s):
        for k in range(TOP_K):
            def body(r, carry, k=k):
                row_copy(0, s, k, r).wait()
                return carry

            lax.fori_loop(0, tm, body, 0, unroll=8)

    @pl.when(j == 0)
    def _():
        for a in range(LOOKAHEAD):
            base = jnp.minimum(a, last) * (tm * TOP_K)
            for k in range(TOP_K):
                def body(r, carry, base=base, a=a, k=k):
                    row_copy(pos_ref[base + r * TOP_K + k], a, k, r).start()
                    return carry

                lax.fori_loop(0, tm, body, 0, unroll=8)

    def step(s):
        drain(s)
        base = jnp.minimum(j + LOOKAHEAD, last) * (tm * TOP_K)
        for r in range(tm):
            for k in range(TOP_K):
                row_copy(pos_ref[base + r * TOP_K + k], (s + LOOKAHEAD) % N_SLOTS, k, r).start()
        ffn = gate_ref[:, 0:1] * ybufs[s][0]
        for k in range(1, TOP_K):
            ffn = ffn + gate_ref[:, k:k + 1] * ybufs[s][k]
        h = h_ref[...]
        ple = jax.nn.sigmoid(_dot(h.astype(BF16), wg_ref[...])) * _dot(p_ref[...].astype(BF16), wp_ref[...])
        out = _ln(alpha * h + ffn + ple, lg_ref[...], lb_ref[...])
        o_ref[...] = out
        ob_ref[...] = out.astype(BF16)

        @pl.when(j == last)
        def _():
            for a in range(1, N_SLOTS):
                drain((s + a) % N_SLOTS)

    for s in range(N_SLOTS):
        pl.when(j % N_SLOTS == s)(functools.partial(step, s))


def _final(h, y_rows, pos, gate, p, ple_w_gate_bf, ple_w_proj_bf, ln_g, ln_b, layer, alpha):
    T = h.shape[0]
    tm = min(T, 128)
    lay2 = lambda t, ps: (layer, 0, 0)
    tile = lambda t, ps: (t, 0)
    return pl.pallas_call(
        functools.partial(_final_kernel, alpha=alpha, tm=tm),
        grid_spec=pltpu.PrefetchScalarGridSpec(
            num_scalar_prefetch=1,
            grid=(T // tm,),
            in_specs=[
                pl.BlockSpec((tm, D_MODEL), tile),
                pl.BlockSpec(memory_space=pl.ANY),
                pl.BlockSpec((tm, TOP_K), tile),
                pl.BlockSpec((None, tm, PLE_DIM), lambda t, ps: (layer, t, 0)),
                pl.BlockSpec((None, D_MODEL, D_MODEL), lay2),
                pl.BlockSpec((None, PLE_DIM, D_MODEL), lay2),
                pl.BlockSpec((None, 1, D_MODEL), lay2),
                pl.BlockSpec((None, 1, D_MODEL), lay2),
            ],
            out_specs=[pl.BlockSpec((tm, D_MODEL), tile)] * 2,
            scratch_shapes=[pltpu.VMEM((TOP_K, tm, D_MODEL), F32)] * N_SLOTS
            + [pltpu.SemaphoreType.DMA((N_SLOTS,))],
        ),
        out_shape=[jax.ShapeDtypeStruct((T, D_MODEL), F32), jax.ShapeDtypeStruct((T, D_MODEL), BF16)],
        compiler_params=_cparams(("arbitrary",), vmem_mb=48),
        name="combine_ple_ln2",
    )(pos, h, y_rows, gate, p, ple_w_gate_bf, ple_w_proj_bf, ln_g, ln_b)


def _moe_route(top_idx, blk):
    T = top_idx.shape[0]
    n_pairs = T * TOP_K
    pos, counts = _route_positions(top_idx, blk)
    pos = pos.reshape(n_pairs)
    pad_end = jnp.cumsum((counts + blk - 1) // blk * blk)
    n_blocks = -(-n_pairs // blk) + N_EXPERTS
    n_rows = n_blocks * blk
    tok = jnp.arange(n_pairs, dtype=jnp.int32) // TOP_K
    row_tok = (jnp.arange(n_rows, dtype=jnp.int32) % T).at[pos].set(
        tok, unique_indices=True, mode='promise_in_bounds')
    block_start = jnp.arange(n_blocks, dtype=jnp.int32) * blk
    block_expert = jnp.minimum(jnp.sum((pad_end[None, :] <= block_start[:, None]).astype(jnp.int32), axis=1),
                               N_EXPERTS - 1)
    n_used = pad_end[-1:].astype(jnp.int32) // blk
    return row_tok, block_expert, n_used, pos


def _moe_ffn_rows(h, top_idx, w_up_bf, b_up, w_down_bf, b_down, layer, blk):
    row_tok, block_expert, n_used, pos = _moe_route(top_idx, blk)
    return _experts(h, row_tok, block_expert, n_used, w_up_bf, b_up, w_down_bf, b_down, layer, blk), pos


def kernel(x_prompt, x_sample, state_ret, cache_dil_k, cache_dil_v, p_prompt, p_sample, w_in, sgu_ln_g, sgu_ln_b, sgu_w, sgu_b, w_branch, w_out, ln1_g, ln1_b, router_w, router_b, exp_w_up, exp_b_up, exp_w_down, exp_b_down, ple_w_gate, ple_w_proj, ln2_g, ln2_b):
    B, S, D = x_prompt.shape
    Bd, C, _ = x_sample.shape
    depth = w_in.shape[0]
    w_buf = cache_dil_k.shape[2]
    alpha = (2 * depth) ** 0.25
    Tp, Ts = B * S, Bd * C
    H, hd = DIL_HEADS, DIL_HD

    w_in_bf, w_branch_bf, w_out_bf = w_in.astype(BF16), w_branch.astype(BF16), w_out.astype(BF16)
    w_up_bf, w_down_bf = exp_w_up.astype(BF16), exp_w_down.astype(BF16)
    ple_w_gate_bf, ple_w_proj_bf = ple_w_gate.astype(BF16), ple_w_proj.astype(BF16)
    row3 = lambda a: a.reshape(depth, 1, a.shape[-1])
    sgu_ln_g3, sgu_ln_b3 = row3(sgu_ln_g), row3(sgu_ln_b)
    ln1_g3, ln1_b3, ln2_g3, ln2_b3 = row3(ln1_g), row3(ln1_b), row3(ln2_g), row3(ln2_b)
    router_b3 = row3(router_b)
    b_up4 = exp_b_up.reshape(depth, N_EXPERTS, 1, 2 * D_FF)
    b_down4 = exp_b_down.reshape(depth, N_EXPERTS, 1, D_MODEL)
    sgu_b4 = sgu_b.reshape(depth, SGU_GROUPS, SGU_CHUNK, 1)
    p_prompt3 = p_prompt.reshape(depth, Tp, PLE_DIM)
    p_sample3 = p_sample.reshape(depth, Ts, PLE_DIM)

    pos_p = jnp.arange(S, dtype=jnp.int32)
    pos_s = PAST_LEN + jnp.arange(C, dtype=jnp.int32)
    ret_cos_p, ret_sin_p = _rot_tables(pos_p, _retention_inv_freq())
    ret_cos_s, ret_sin_s = _rot_tables(pos_s, _retention_inv_freq())
    dil_cos_p, dil_sin_p = _rot_tables(pos_p, _rope_inv_freq(hd))
    dil_cos_s, dil_sin_s = _rot_tables(pos_s, _rope_inv_freq(hd))
    ret_tabs_p = _retention_tables(RET_CHUNK)
    ret_tabs_s = _retention_tables(C)
    cnt_a, cnt_b, cnt_n, tail, period = _sample_key_tables(C, w_buf)
    key_tables = (jnp.asarray(cnt_a), jnp.asarray(cnt_b), jnp.asarray(cnt_n), tail, period)

    xp, xs = x_prompt.reshape(Tp, D), x_sample.reshape(Ts, D)
    xp_bf, xs_bf = xp.astype(BF16), xs.astype(BF16)
    ret_p, ret_s, k_p, v_p, k_s, v_s, sgu_s = [], [], [], [], [], [], []

    def channel_mix(x, out_a, out_b, out_c, cols, p3, i, blk):
        h, top_idx, gate = _merge(out_a, out_b, out_c, cols, x, w_branch_bf, w_out_bf, ln1_g3, ln1_b3,
                                  router_w, router_b3, i, alpha)
        y_rows, pos = _moe_ffn_rows(h, top_idx, w_up_bf, b_up4, w_down_bf, b_down4, i, blk)
        return _final(h, y_rows, pos, gate, p3, ple_w_gate_bf, ple_w_proj_bf, ln2_g3, ln2_b3, i, alpha)

    heads_last = lambda a: a.reshape(Bd, H, C, hd).transpose(0, 2, 1, 3)

    for i in range(depth):
        cols = _in_proj(xp_bf, w_in_bf, i)
        out_a, st = _retention_prompt(cols, B, S, ret_cos_p, ret_sin_p, ret_tabs_p)
        out_b = _sgu_prompt(cols, Tp, sgu_ln_g3, sgu_ln_b3, sgu_w, sgu_b4, i)
        out_c, kr, vr = _dilated_prompt(cols, B, S, dil_cos_p, dil_sin_p)
        xp, xp_bf = channel_mix(xp, out_a, out_b, out_c, cols, p_prompt3, i, MOE_ROWS_PROMPT)
        ret_p.append(st)
        k_p.append(kr.reshape(B, S, H, hd))
        v_p.append(vr.reshape(B, S, H, hd))
        cols = _in_proj(xs_bf, w_in_bf, i)
        cols3 = cols.reshape(Bd, C, N_IN_COLS)
        out_a, out_b, st, vn = _ret_sgu_sample(cols3, state_ret, ret_cos_s, ret_sin_s, ret_tabs_s,
                                               sgu_ln_g3, sgu_ln_b3, sgu_w, sgu_b4, i)
        out_c, kr, vr = _dilated_sample(cols3, cache_dil_k, cache_dil_v, dil_cos_s, dil_sin_s, key_tables, i)
        xs, xs_bf = channel_mix(xs, out_a.reshape(Ts, D), out_b.reshape(Ts, D), heads_last(out_c).reshape(Ts, D),
                                cols, p_sample3, i, MOE_ROWS_SAMPLE)
        ret_s.append(st)
        k_s.append(heads_last(kr))
        v_s.append(heads_last(vr))
        sgu_s.append(vn)

    return (xp.reshape(B, S, D), xs.reshape(Bd, C, D), jnp.stack(ret_p), jnp.stack(ret_s),
            jnp.stack(k_p), jnp.stack(v_p),
            jnp.stack(k_s), jnp.stack(v_s), jnp.stack(sgu_s))
```

```python
import functools

import numpy as np
import jax
import jax.numpy as jnp
from jax import lax
from jax.experimental import pallas as pl
from jax.experimental.pallas import tpu as pltpu

F32, BF16 = jnp.float32, jnp.bfloat16

D_MODEL = 1024
PAST_LEN = 8192
RET_HEADS, RET_QK, RET_V, RET_CHUNK = 4, 128, 256, 128
SGU_GROUPS, SGU_CHUNK = 4, 128
SGU_GW = D_MODEL // SGU_GROUPS
DIL_HEADS, DIL_HD, DIL_BLOCK = 8, 128, 128
DIL_PATTERNS = ((128, 1), (512, 4), (2048, 16))
ROPE_THETA = 10000.0
N_BRANCHES = 3
N_EXPERTS, TOP_K, D_FF = 32, 4, 1024
SWIGLU_ALPHA, SWIGLU_LIMIT = 1.702, 7.0
PLE_DIM = 256
LN_EPS = 1e-5

C_AQ, C_AK, C_AV, C_AG = 0, 512, 1024, 2048
C_BU, C_BV = 3072, 4096
C_CQ, C_CK, C_CV = 5120, 6144, 7168
C_GATE = 8192
N_IN_COLS = C_GATE + N_BRANCHES * D_MODEL

LANES = 128
SUBLANES = 8
MOE_ROWS = 256


def _cparams(sem, vmem_mb=None):
    kw = dict(dimension_semantics=sem)
    if vmem_mb is not None:
        kw["vmem_limit_bytes"] = vmem_mb * 2 ** 20
    return pltpu.CompilerParams(**kw)


def _dot(a, b):
    return jnp.dot(a, b, preferred_element_type=F32)


def _dot_nt(a, b):
    return lax.dot_general(a, b, (((1,), (1,)), ((), ())), preferred_element_type=F32)


def _dot_tn(a, b):
    return lax.dot_general(a, b, (((0,), (0,)), ((), ())), preferred_element_type=F32)


def _ln(x, g, b):
    mu = jnp.mean(x, axis=-1, keepdims=True)
    xc = x - mu
    var = jnp.mean(xc * xc, axis=-1, keepdims=True)
    return xc * lax.rsqrt(var + LN_EPS) * g + b


def _rot(x, cosf, sinf):
    return x * cosf + pltpu.roll(x, LANES // 2, 1) * sinf


def _rot_tables(pos, inv_freq):
    ang = pos.astype(F32)[:, None] * inv_freq[None, :]
    cos, sin = jnp.cos(ang), jnp.sin(ang)
    return jnp.concatenate([cos, cos], axis=-1), jnp.concatenate([-sin, sin], axis=-1)


def _retention_inv_freq():
    return 1.0 / (10000.0 ** jnp.linspace(0.0, 1.0, RET_QK // 2, dtype=F32))


def _rope_inv_freq(hd):
    return ROPE_THETA ** (-jnp.arange(0, hd, 2, dtype=F32) / hd)


def _retention_tables(C):
    log_g = jnp.log1p(-jnp.exp2(-5.0 - jnp.arange(RET_HEADS, dtype=F32)))
    n = jnp.arange(C, dtype=F32)
    diff = n[:, None] - n[None, :]
    decay = jnp.where(diff >= 0, jnp.exp(log_g[:, None, None] * jnp.maximum(diff, 0.0)), 0.0)
    qscale = jnp.exp(log_g[:, None] * (n[None, :] + 1.0))[:, :, None]
    kscale = jnp.exp(log_g[:, None] * (C - 1.0 - n[None, :]))[:, :, None]
    gc = jnp.exp(log_g * C)[:, None, None]
    return decay, qscale, kscale, gc


def _mm_kernel(x_ref, w_ref, o_ref):
    o_ref[...] = _dot(x_ref[...], w_ref[...]).astype(o_ref.dtype)


def _in_proj(x_bf, w_in_bf, layer):
    T = x_bf.shape[0]
    tm = min(T, 2048)
    tn = 1024
    return pl.pallas_call(
        _mm_kernel,
        grid=(N_IN_COLS // tn, T // tm),
        in_specs=[pl.BlockSpec((tm, D_MODEL), lambda n, t: (t, 0)),
                  pl.BlockSpec((None, D_MODEL, tn), lambda n, t: (layer, 0, n))],
        out_specs=pl.BlockSpec((tm, tn), lambda n, t: (t, n)),
        out_shape=jax.ShapeDtypeStruct((T, N_IN_COLS), BF16),
        compiler_params=_cparams(("parallel", "arbitrary")),
        name="in_proj",
    )(x_bf, w_in_bf)


def _group_norm_gate(o, g):
    mu = jnp.mean(o, axis=-1, keepdims=True)
    oc = o - mu
    var = jnp.mean(oc * oc, axis=-1, keepdims=True)
    return oc * lax.rsqrt(var + LN_EPS) * jax.nn.silu(g)


def _ret_prompt_kernel(q_ref, k_ref, v_ref, g_ref, cos_ref, sin_ref, decay_ref, qs_ref, ks_ref, gc_ref,
                       o_ref, st_ref, state, *, n_chunks):
    c = pl.program_id(1)

    @pl.when(c == 0)
    def _():
        state[...] = jnp.zeros_like(state)

    cosf, sinf = cos_ref[...], sin_ref[...]
    for h in range(RET_HEADS):
        qk = slice(h * RET_QK, (h + 1) * RET_QK)
        vv = slice(h * RET_V, (h + 1) * RET_V)
        q = _rot(q_ref[:, qk].astype(F32), cosf, sinf)
        k = _rot(k_ref[:, qk].astype(F32), cosf, sinf) * RET_QK ** -0.5
        v = v_ref[:, vv]
        qb = q.astype(BF16)
        s = _dot_nt(qb, k.astype(BF16)) * decay_ref[h]
        o = _dot(s.astype(BF16), v)
        st = state[h]
        o = o + _dot(qb, st.astype(BF16)) * qs_ref[h]
        kw = (k * ks_ref[h]).astype(BF16)
        state[h] = st * gc_ref[h] + _dot_tn(kw, v)
        o_ref[:, vv] = _group_norm_gate(o, g_ref[:, vv].astype(F32)).astype(o_ref.dtype)

    @pl.when(c == n_chunks - 1)
    def _():
        st_ref[...] = state[...]


def _retention_prompt(cols, B, S, cosf, sinf, tabs):
    C = RET_CHUNK
    nc = S // C
    decay, qscale, kscale, gc = tabs
    H = RET_HEADS
    row = lambda b, c: b * nc + c
    return pl.pallas_call(
        functools.partial(_ret_prompt_kernel, n_chunks=nc),
        grid=(B, nc),
        in_specs=[
            pl.BlockSpec((C, H * RET_QK), lambda b, c: (row(b, c), C_AQ // (H * RET_QK))),
            pl.BlockSpec((C, H * RET_QK), lambda b, c: (row(b, c), C_AK // (H * RET_QK))),
            pl.BlockSpec((C, H * RET_V), lambda b, c: (row(b, c), C_AV // (H * RET_V))),
            pl.BlockSpec((C, H * RET_V), lambda b, c: (row(b, c), C_AG // (H * RET_V))),
            pl.BlockSpec((C, RET_QK), lambda b, c: (c, 0)),
            pl.BlockSpec((C, RET_QK), lambda b, c: (c, 0)),
            pl.BlockSpec((H, C, C), lambda b, c: (0, 0, 0)),
            pl.BlockSpec((H, C, 1), lambda b, c: (0, 0, 0)),
            pl.BlockSpec((H, C, 1), lambda b, c: (0, 0, 0)),
            pl.BlockSpec((H, 1, 1), lambda b, c: (0, 0, 0)),
        ],
        out_specs=[
            pl.BlockSpec((C, H * RET_V), lambda b, c: (row(b, c), 0)),
            pl.BlockSpec((None, H, RET_QK, RET_V), lambda b, c: (b, 0, 0, 0)),
        ],
        out_shape=[jax.ShapeDtypeStruct((B * S, H * RET_V), BF16),
                   jax.ShapeDtypeStruct((B, H, RET_QK, RET_V), F32)],
        scratch_shapes=[pltpu.VMEM((H, RET_QK, RET_V), F32)],
        compiler_params=_cparams(("parallel", "arbitrary")),
        name="retention_prompt",
    )(cols, cols, cols, cols, cosf, sinf, decay, qscale, kscale, gc)


def _sgu_weights(w_ref):
    r = lax.broadcasted_iota(jnp.int32, (SGU_CHUNK, SGU_CHUNK), 0)
    c = lax.broadcasted_iota(jnp.int32, (SGU_CHUNK, SGU_CHUNK), 1)
    return [jnp.where(r >= c, w_ref[g], 0.0) for g in range(SGU_GROUPS)]


def _sgu_prompt_kernel(u_ref, v_ref, lg_ref, lb_ref, w_ref, bs_ref, o_ref, *, n_chunks):
    w = [x.astype(BF16) for x in _sgu_weights(w_ref)]
    for c in range(n_chunks):
        rows = slice(c * SGU_CHUNK, (c + 1) * SGU_CHUNK)
        u = jax.nn.gelu(u_ref[rows, :].astype(F32))
        vn = _ln(jax.nn.gelu(v_ref[rows, :].astype(F32)), lg_ref[...], lb_ref[...])
        for g in range(SGU_GROUPS):
            cs = slice(g * SGU_GW, (g + 1) * SGU_GW)
            f = _dot(w[g], vn[:, cs].astype(BF16)) + bs_ref[g]
            o_ref[rows, cs] = (u[:, cs] * f).astype(o_ref.dtype)


def _sgu_prompt(cols, T, sgu_ln_g, sgu_ln_b, sgu_w, sgu_b4, layer):
    n_chunks = 4
    R = n_chunks * SGU_CHUNK
    return pl.pallas_call(
        functools.partial(_sgu_prompt_kernel, n_chunks=n_chunks),
        grid=(T // R,),
        in_specs=[
            pl.BlockSpec((R, D_MODEL), lambda t: (t, C_BU // D_MODEL)),
            pl.BlockSpec((R, D_MODEL), lambda t: (t, C_BV // D_MODEL)),
            pl.BlockSpec((None, 1, D_MODEL), lambda t: (layer, 0, 0)),
            pl.BlockSpec((None, 1, D_MODEL), lambda t: (layer, 0, 0)),
            pl.BlockSpec((None, SGU_GROUPS, SGU_CHUNK, SGU_CHUNK), lambda t: (layer, 0, 0, 0)),
            pl.BlockSpec((None, SGU_GROUPS, SGU_CHUNK, 1), lambda t: (layer, 0, 0, 0)),
        ],
        out_specs=pl.BlockSpec((R, D_MODEL), lambda t: (t, 0)),
        out_shape=jax.ShapeDtypeStruct((T, D_MODEL), BF16),
        compiler_params=_cparams(("parallel",)),
        name="sgu_prompt",
    )(cols, cols, sgu_ln_g, sgu_ln_b, sgu_w, sgu_b4)


def _dil_prompt_kernel(q_ref, k_ref, v_ref, cos_ref, sin_ref, o_ref, ko_ref, vo_ref, qp, kp, vp, o_p, lse_p, *, S):
    BLK, hd = DIL_BLOCK, DIL_HD
    nblk = S // BLK
    scale = hd ** -0.5
    RC = 256

    def rot_body(j, carry):
        r = pl.ds(pl.multiple_of(j * RC, RC), RC)
        cosf, sinf = cos_ref[r, :], sin_ref[r, :]
        k = _rot(k_ref[r, :].astype(F32), cosf, sinf)
        v = v_ref[r, :].astype(F32)
        qp[r, :] = _rot(q_ref[r, :].astype(F32), cosf, sinf)
        kp[r, :] = k
        vp[r, :] = v
        ko_ref[r, :] = k
        vo_ref[r, :] = v
        return carry

    lax.fori_loop(0, S // RC, rot_body, 0)

    qi = lax.broadcasted_iota(jnp.int32, (BLK, BLK), 0)
    kj = lax.broadcasted_iota(jnp.int32, (BLK, BLK), 1)
    lower = kj <= qi
    upper = kj >= qi
    ones_v = jnp.ones((BLK, hd), BF16)

    def block(p, d, j):
        nps = nblk // d
        if d == 1:
            n, start = j, pl.multiple_of(j * BLK, BLK)
            rows = lambda st: pl.ds(pl.multiple_of(st, BLK), BLK)
        else:
            n = j % nps
            start = j // nps + n * (BLK * d)
            rows = lambda st: pl.ds(st, BLK, stride=d)
        v_aug = lambda st: jnp.concatenate([vp[rows(st), :].astype(BF16), ones_v], axis=1)
        qb = qp[rows(start), :].astype(BF16)
        s_c = jnp.where(lower, _dot_nt(qb, kp[rows(start), :].astype(BF16)) * scale, -jnp.inf)
        m = jnp.max(s_c, axis=-1, keepdims=True)
        if nps > 1:
            prev = jnp.where(n > 0, start - BLK * d, start)
            no_prev = jnp.where(n > 0, 0.0, -jnp.inf)
            s_p = jnp.where(upper, _dot_nt(qb, kp[rows(prev), :].astype(BF16)) * scale + no_prev, -jnp.inf)
            m = jnp.maximum(m, jnp.max(s_p, axis=-1, keepdims=True))
        acc = _dot(jnp.exp(s_c - m).astype(BF16), v_aug(start))
        if nps > 1:
            acc = acc + _dot(jnp.exp(s_p - m).astype(BF16), v_aug(prev))
        den = acc[:, hd:]
        o_p[p, rows(start), :] = acc[:, :hd] / den
        lse_p[p, rows(start), :] = m + jnp.log(den)

    def blocks(j, carry):
        for p, (_, d) in enumerate(DIL_PATTERNS):
            block(p, d, j)
        return carry

    lax.fori_loop(0, nblk, blocks, 0, unroll=4)

    def combine(j, carry):
        r = pl.ds(pl.multiple_of(j * BLK, BLK), BLK)
        lses = [lse_p[p, r, :] for p in range(len(DIL_PATTERNS))]
        m = functools.reduce(jnp.maximum, lses)
        ws = [jnp.exp(l - m) for l in lses]
        num = sum(w * o_p[p, r, :] for p, w in enumerate(ws))
        o_ref[r, :] = (num / sum(ws)).astype(o_ref.dtype)
        return carry

    lax.fori_loop(0, nblk, combine, 0)


def _dilated_prompt(cols, B, S, cosf, sinf):
    H, hd = DIL_HEADS, DIL_HD
    P = len(DIL_PATTERNS)
    assert all(w // d == DIL_BLOCK for w, d in DIL_PATTERNS) and DIL_PATTERNS[0][1] == 1
    assert all((S // DIL_BLOCK) % d == 0 for _, d in DIL_PATTERNS)
    col = lambda off: (lambda b, h: (b, off // hd + h))
    vm = lambda shape, dt: pltpu.VMEM(shape, dt)
    return pl.pallas_call(
        functools.partial(_dil_prompt_kernel, S=S),
        grid=(B, H),
        in_specs=[
            pl.BlockSpec((S, hd), col(C_CQ)),
            pl.BlockSpec((S, hd), col(C_CK)),
            pl.BlockSpec((S, hd), col(C_CV)),
            pl.BlockSpec((S, hd), lambda b, h: (0, 0)),
            pl.BlockSpec((S, hd), lambda b, h: (0, 0)),
        ],
        out_specs=[pl.BlockSpec((S, hd), lambda b, h: (b, h))] * 3,
        out_shape=[jax.ShapeDtypeStruct((B * S, H * hd), BF16),
                   jax.ShapeDtypeStruct((B * S, H * hd), F32),
                   jax.ShapeDtypeStruct((B * S, H * hd), F32)],
        scratch_shapes=[vm((S, hd), F32)] * 3 + [vm((P, S, hd), F32), vm((P, S, hd), F32)],
        compiler_params=_cparams(("parallel", "arbitrary"), vmem_mb=40),
        name="dilated_prompt",
    )(cols, cols, cols, cosf, sinf)


def _ret_sgu_sample_kernel(cols_ref, st_ref, cos_ref, sin_ref, decay_ref, qs_ref, ks_ref, gc_ref,
                           lg_ref, lb_ref, w_ref, bs_ref, oa_ref, ob_ref, sto_ref, vn_ref, *, C):
    cosf, sinf = cos_ref[...], sin_ref[...]
    for h in range(RET_HEADS):
        q = _rot(cols_ref[:, C_AQ + h * RET_QK:C_AQ + (h + 1) * RET_QK].astype(F32), cosf, sinf)
        k = _rot(cols_ref[:, C_AK + h * RET_QK:C_AK + (h + 1) * RET_QK].astype(F32), cosf, sinf) * RET_QK ** -0.5
        v = cols_ref[:, C_AV + h * RET_V:C_AV + (h + 1) * RET_V].astype(F32)
        g = cols_ref[:, C_AG + h * RET_V:C_AG + (h + 1) * RET_V].astype(F32)
        s = _dot_nt(q.astype(BF16), k.astype(BF16)) * decay_ref[h]
        o = s[:, 0:1] * v[0:1, :]
        for j in range(1, C):
            o = o + s[:, j:j + 1] * v[j:j + 1, :]
        st = st_ref[h]
        o = o + _dot(q.astype(BF16), st.astype(BF16)) * qs_ref[h]
        kw = (k * ks_ref[h]).astype(BF16)
        sto_ref[h] = st * gc_ref[h] + _dot_tn(kw, v.astype(BF16))
        oa_ref[:, h * RET_V:(h + 1) * RET_V] = _group_norm_gate(o, g).astype(oa_ref.dtype)
    u = jax.nn.gelu(cols_ref[:, C_BU:C_BU + D_MODEL].astype(F32))
    vn = _ln(jax.nn.gelu(cols_ref[:, C_BV:C_BV + D_MODEL].astype(F32)), lg_ref[...], lb_ref[...])
    vn_ref[...] = vn
    w = _sgu_weights(w_ref)
    for g in range(SGU_GROUPS):
        cs = slice(g * SGU_GW, (g + 1) * SGU_GW)
        wg = w[g][:C, :C]
        f = bs_ref[g][:C, :]
        for j in range(C):
            f = f + wg[:, j:j + 1] * vn[j:j + 1, cs]
        ob_ref[:, cs] = (u[:, cs] * f).astype(ob_ref.dtype)


def _ret_sgu_sample(cols3, state_ret, cosf, sinf, tabs, sgu_ln_g, sgu_ln_b, sgu_w, sgu_b4, layer):
    Bd, C, _ = cols3.shape
    decay, qscale, kscale, gc = tabs
    H = RET_HEADS
    const3 = lambda b: (0, 0, 0)
    return pl.pallas_call(
        functools.partial(_ret_sgu_sample_kernel, C=C),
        grid=(Bd,),
        in_specs=[
            pl.BlockSpec((None, C, N_IN_COLS), lambda b: (b, 0, 0)),
            pl.BlockSpec((None, None, H, RET_QK, RET_V), lambda b: (layer, b, 0, 0, 0)),
            pl.BlockSpec((C, RET_QK), lambda b: (0, 0)),
            pl.BlockSpec((C, RET_QK), lambda b: (0, 0)),
            pl.BlockSpec((H, C, C), const3),
            pl.BlockSpec((H, C, 1), const3),
            pl.BlockSpec((H, C, 1), const3),
            pl.BlockSpec((H, 1, 1), const3),
            pl.BlockSpec((None, 1, D_MODEL), lambda b: (layer, 0, 0)),
            pl.BlockSpec((None, 1, D_MODEL), lambda b: (layer, 0, 0)),
            pl.BlockSpec((None, SGU_GROUPS, SGU_CHUNK, SGU_CHUNK), lambda b: (layer, 0, 0, 0)),
            pl.BlockSpec((None, SGU_GROUPS, SGU_CHUNK, 1), lambda b: (layer, 0, 0, 0)),
        ],
        out_specs=[
            pl.BlockSpec((None, C, D_MODEL), lambda b: (b, 0, 0)),
            pl.BlockSpec((None, C, D_MODEL), lambda b: (b, 0, 0)),
            pl.BlockSpec((None, H, RET_QK, RET_V), lambda b: (b, 0, 0, 0)),
            pl.BlockSpec((None, C, D_MODEL), lambda b: (b, 0, 0)),
        ],
        out_shape=[jax.ShapeDtypeStruct((Bd, C, D_MODEL), BF16),
                   jax.ShapeDtypeStruct((Bd, C, D_MODEL), BF16),
                   jax.ShapeDtypeStruct((Bd, H, RET_QK, RET_V), F32),
                   jax.ShapeDtypeStruct((Bd, C, D_MODEL), F32)],
        compiler_params=_cparams(("parallel",)),
        name="ret_sgu_sample",
    )(cols3, state_ret, cosf, sinf, decay, qscale, kscale, gc, sgu_ln_g, sgu_ln_b, sgu_w, sgu_b4)


def _sample_key_tables(C, w_buf):
    H = DIL_HEADS
    per_pattern = []
    for window, dil in DIL_PATTERNS:
        past, new = np.zeros((C, w_buf), np.float32), np.zeros((C, C), np.float32)
        for t in range(C):
            for j in range(window // dil + 1):
                idx = w_buf + t - dil * j
                assert idx >= 0
                if idx < w_buf:
                    past[t, idx] += 1
                else:
                    new[t, idx - w_buf] += 1
        per_pattern.append((past, new))
    tail = max(w for w, _ in DIL_PATTERNS[:-1])
    period = DIL_PATTERNS[-1][1]
    past_tail = sum(p for p, _ in per_pattern[:-1])
    past_wide = per_pattern[-1][0]
    in_wide = (np.arange(w_buf) % period) < C
    assert w_buf % tail == 0 and w_buf % period == 0
    assert not past_tail[:, :w_buf - tail].any() and not past_wide[:, ~in_wide].any()
    new = sum(n for _, n in per_pattern)

    def expand(cnt):
        out = np.zeros((H, C, cnt.shape[1], H), np.float32)
        for h in range(H):
            out[h, :, :, h] = cnt
        return out.reshape(H * C, cnt.shape[1] * H)

    cnt_n = np.zeros((H, C, H, C), np.float32)
    for h in range(H):
        cnt_n[h, :, h, :] = new
    return expand(past_tail[:, w_buf - tail:]), expand(past_wide[:, in_wide]), cnt_n.reshape(H * C, H * C), tail, period


def _dil_sample_kernel(q_ref, k_ref, v_ref, ka_ref, va_ref, kb_ref, vb_ref, cos_ref, sin_ref,
                       ca_ref, cb_ref, cn_ref, o_ref, ko_ref, vo_ref, s_scr, *, C):
    H, hd = DIL_HEADS, DIL_HD
    scale = hd ** -0.5
    KC = 1024
    cosf, sinf = cos_ref[...], sin_ref[...]
    heads = [slice(h * hd, (h + 1) * hd) for h in range(H)]
    q_all, k_all, v_all = q_ref[...].astype(F32), k_ref[...].astype(F32), v_ref[...].astype(F32)
    q_rows = jnp.concatenate([_rot(q_all[:, cs], cosf, sinf) for cs in heads], axis=0)
    k_new = jnp.concatenate([_rot(k_all[:, cs], cosf, sinf) for cs in heads], axis=0)
    v_new = jnp.concatenate([v_all[:, cs] for cs in heads], axis=0)
    ko_ref[...] = k_new
    vo_ref[...] = v_new
    qb = q_rows.astype(BF16)

    na, nb = ka_ref.shape[0] * H // KC, kb_ref.shape[0] * C * H // KC
    ra, rb = KC // H, KC // (C * H)
    chunks = [(ka_ref, va_ref, ca_ref, c, slice(c * ra, (c + 1) * ra)) for c in range(na)]
    chunks += [(kb_ref, vb_ref, cb_ref, c, slice(c * rb, (c + 1) * rb)) for c in range(nb)]

    cnt_n = cn_ref[...]
    s_n = jnp.where(cnt_n > 0, _dot_nt(qb, k_new.astype(BF16)) * scale, -jnp.inf)
    m = jnp.max(s_n, axis=-1, keepdims=True)
    for i, (kref, _, cref, c, rows) in enumerate(chunks):
        cols = slice(c * KC, (c + 1) * KC)
        s = _dot_nt(qb, kref[rows].reshape(KC, hd).astype(BF16)) * scale
        s = jnp.where(cref[:, cols] > 0, s, -jnp.inf)
        s_scr[:, i * KC:(i + 1) * KC] = s
        m = jnp.maximum(m, jnp.max(s, axis=-1, keepdims=True))
    e_n = cnt_n * jnp.exp(s_n - m)
    l = jnp.sum(e_n, axis=-1, keepdims=True)
    acc = _dot(e_n.astype(BF16), v_new.astype(BF16))
    for i, (_, vref, cref, c, rows) in enumerate(chunks):
        cols = slice(c * KC, (c + 1) * KC)
        e = cref[:, cols] * jnp.exp(s_scr[:, i * KC:(i + 1) * KC] - m)
        l = l + jnp.sum(e, axis=-1, keepdims=True)
        acc = acc + _dot(e.astype(BF16), vref[rows].reshape(KC, hd).astype(BF16))
    o_ref[...] = (acc / l).astype(o_ref.dtype)


def _dilated_sample(cols3, cache_k, cache_v, cosf, sinf, tables, layer):
    Bd, C, _ = cols3.shape
    depth, _, w_buf, H, hd = cache_k.shape
    cnt_a, cnt_b, cnt_n, tail, period = tables
    W = H * hd
    tail_view = lambda a: a.reshape(depth, Bd, w_buf // tail, tail, H, hd)
    wide_view = lambda a: a.reshape(depth, Bd, w_buf // period, period, H, hd)
    colblk = lambda off: (lambda b: (b, 0, off // W))
    const2 = lambda b: (0, 0)
    tail_spec = pl.BlockSpec((None, None, None, tail, H, hd), lambda b: (layer, b, w_buf // tail - 1, 0, 0, 0))
    wide_spec = pl.BlockSpec((None, None, w_buf // period, C, H, hd), lambda b: (layer, b, 0, 0, 0, 0))
    n_keys = (tail + w_buf // period * C) * H
    return pl.pallas_call(
        functools.partial(_dil_sample_kernel, C=C),
        grid=(Bd,),
        in_specs=[
            pl.BlockSpec((None, C, W), colblk(C_CQ)),
            pl.BlockSpec((None, C, W), colblk(C_CK)),
            pl.BlockSpec((None, C, W), colblk(C_CV)),
            tail_spec, tail_spec, wide_spec, wide_spec,
            pl.BlockSpec((C, hd), const2),
            pl.BlockSpec((C, hd), const2),
            pl.BlockSpec(cnt_a.shape, const2),
            pl.BlockSpec(cnt_b.shape, const2),
            pl.BlockSpec(cnt_n.shape, const2),
        ],
        out_specs=[pl.BlockSpec((None, H * C, hd), lambda b: (b, 0, 0))] * 3,
        out_shape=[jax.ShapeDtypeStruct((Bd, H * C, hd), BF16),
                   jax.ShapeDtypeStruct((Bd, H * C, hd), F32),
                   jax.ShapeDtypeStruct((Bd, H * C, hd), F32)],
        scratch_shapes=[pltpu.VMEM((H * C, n_keys), F32)],
        compiler_params=_cparams(("parallel",), vmem_mb=40),
        name="dilated_sample",
    )(cols3, cols3, cols3, tail_view(cache_k), tail_view(cache_v), wide_view(cache_k), wide_view(cache_v),
      cosf, sinf, cnt_a, cnt_b, cnt_n)


def _merge_kernel(a_ref, b_ref, c_ref, g0_ref, g1_ref, g2_ref, x_ref, wb_ref, wo_ref, lg_ref, lb_ref,
                  rw_ref, rb_ref, h_ref, idx_ref, gate_ref, *, alpha):
    merged = None
    for n, (br, gt) in enumerate(((a_ref, g0_ref), (b_ref, g1_ref), (c_ref, g2_ref))):
        term = jax.nn.sigmoid(gt[...].astype(F32)) * _dot(br[...], wb_ref[n])
        merged = term if merged is None else merged + term
    y = _dot(merged.astype(BF16), wo_ref[...])
    h = _ln(alpha * x_ref[...] + y, lg_ref[...], lb_ref[...])
    h_ref[...] = h
    h_hi = h.astype(BF16)
    h_lo = (h - h_hi.astype(F32)).astype(BF16)
    rw = rw_ref[...]
    rw_hi = rw.astype(BF16)
    rw_lo = (rw - rw_hi.astype(F32)).astype(BF16)
    logits = _dot(h_hi, rw_hi) + _dot(h_lo, rw_hi) + _dot(h_hi, rw_lo) + rb_ref[...]
    lane = lax.broadcasted_iota(jnp.int32, logits.shape, 1)
    vals, idxs = [], []
    for _ in range(TOP_K):
        mx = jnp.max(logits, axis=-1, keepdims=True)
        am = jnp.min(jnp.where(logits == mx, lane, N_EXPERTS), axis=-1, keepdims=True)
        vals.append(mx)
        idxs.append(am)
        logits = jnp.where(lane == am, -jnp.inf, logits)
    e = jnp.exp(jnp.concatenate(vals, axis=-1) - vals[0])
    idx_ref[...] = jnp.concatenate(idxs, axis=-1)
    gate_ref[...] = e / jnp.sum(e, axis=-1, keepdims=True)


def _merge(out_a, out_b, out_c, cols, x, w_branch_bf, w_out_bf, ln_g, ln_b, router_w, router_b, layer, alpha):
    T = x.shape[0]
    tm = min(T, 512)
    tile = lambda j: (lambda t: (t, j))
    lay2 = lambda t: (layer, 0, 0)
    return pl.pallas_call(
        functools.partial(_merge_kernel, alpha=alpha),
        grid=(T // tm,),
        in_specs=[
            pl.BlockSpec((tm, D_MODEL), tile(0)),
            pl.BlockSpec((tm, D_MODEL), tile(0)),
            pl.BlockSpec((tm, D_MODEL), tile(0)),
            pl.BlockSpec((tm, D_MODEL), tile(C_GATE // D_MODEL)),
            pl.BlockSpec((tm, D_MODEL), tile(C_GATE // D_MODEL + 1)),
            pl.BlockSpec((tm, D_MODEL), tile(C_GATE // D_MODEL + 2)),
            pl.BlockSpec((tm, D_MODEL), tile(0)),
            pl.BlockSpec((None, N_BRANCHES, D_MODEL, D_MODEL), lambda t: (layer, 0, 0, 0)),
            pl.BlockSpec((None, D_MODEL, D_MODEL), lay2),
            pl.BlockSpec((None, 1, D_MODEL), lay2),
            pl.BlockSpec((None, 1, D_MODEL), lay2),
            pl.BlockSpec((None, D_MODEL, N_EXPERTS), lay2),
            pl.BlockSpec((None, 1, N_EXPERTS), lay2),
        ],
        out_specs=[pl.BlockSpec((tm, D_MODEL), tile(0)), pl.BlockSpec((tm, TOP_K), tile(0)),
                   pl.BlockSpec((tm, TOP_K), tile(0))],
        out_shape=[jax.ShapeDtypeStruct((T, D_MODEL), F32), jax.ShapeDtypeStruct((T, TOP_K), jnp.int32),
                   jax.ShapeDtypeStruct((T, TOP_K), F32)],
        compiler_params=_cparams(("parallel",), vmem_mb=48),
        name="merge_ln1_router",
    )(out_a, out_b, out_c, cols, cols, cols, x, w_branch_bf, w_out_bf, ln_g, ln_b, router_w, router_b)


def _route_kernel(idx_ref, pos_ref, cnt_ref, tot, run, *, blk_log2):
    phase, t = pl.program_id(0), pl.program_id(1)
    tm = idx_ref.shape[0]
    idx = idx_ref[...]
    lane = lax.broadcasted_iota(jnp.int32, (tm, LANES), 1)
    ohs = [(idx[:, k:k + 1] == lane).astype(F32) for k in range(TOP_K)]
    oh = functools.reduce(lambda a, b: a + b, ohs)
    colsum = jnp.sum(oh, axis=0, keepdims=True)

    @pl.when((phase == 0) & (t == 0))
    def _():
        tot[...] = jnp.zeros_like(tot)

    @pl.when(phase == 0)
    def _():
        tot[...] += colsum

    @pl.when((phase == 1) & (t == 0))
    def _():
        counts = tot[...].astype(jnp.int32)
        blk = 1 << blk_log2
        padded = lax.shift_left(lax.shift_right_logical(counts + (blk - 1), blk_log2), blk_log2).astype(F32)
        lane8 = lax.broadcasted_iota(jnp.int32, tot.shape, 1)
        incl = padded
        shift = 1
        while shift < N_EXPERTS:
            incl = incl + jnp.where(lane8 >= shift, pltpu.roll(incl, shift, 1), 0.0)
            shift *= 2
        run[...] = incl - padded

    @pl.when(phase == 1)
    def _():
        r = lax.broadcasted_iota(jnp.int32, (tm, tm), 0)
        c = lax.broadcasted_iota(jnp.int32, (tm, tm), 1)
        before = _dot((c < r).astype(BF16), oh.astype(BF16)) + run[0:1, :]
        pos = jnp.concatenate([jnp.sum(o * before, axis=-1, keepdims=True) for o in ohs], axis=-1)
        pos_ref[...] = pos.astype(jnp.int32)
        run[...] += colsum
        cnt_ref[...] = tot[...].astype(jnp.int32)


def _route_positions(top_idx, blk):
    T = top_idx.shape[0]
    tm = max(t for t in range(SUBLANES, min(T, 512) + 1, SUBLANES) if T % t == 0)
    assert blk & (blk - 1) == 0 and N_EXPERTS <= LANES
    pos, cnt = pl.pallas_call(
        functools.partial(_route_kernel, blk_log2=blk.bit_length() - 1),
        grid=(2, T // tm),
        in_specs=[pl.BlockSpec((tm, TOP_K), lambda ph, t: (t, 0))],
        out_specs=[pl.BlockSpec((tm, TOP_K), lambda ph, t: (t * ph, 0)),
                   pl.BlockSpec((SUBLANES, LANES), lambda ph, t: (0, 0))],
        out_shape=[jax.ShapeDtypeStruct((T, TOP_K), jnp.int32),
                   jax.ShapeDtypeStruct((SUBLANES, LANES), jnp.int32)],
        scratch_shapes=[pltpu.VMEM((SUBLANES, LANES), F32)] * 2,
        compiler_params=_cparams(("arbitrary", "arbitrary")),
        name="route_positions",
    )(top_idx)
    return pos, cnt[0, :N_EXPERTS]


LOOKAHEAD = 2
N_SLOTS = LOOKAHEAD + 1


def _expert_kernel(be_ref, nu_ref, rt_ref, h_ref, wu_ref, bu_ref, wd_ref, bd_ref, o_ref, *scratch, blk):
    xbufs, (sem, wu_b, wd_b) = scratch[:N_SLOTS], scratch[N_SLOTS:]
    j = pl.program_id(0)
    last = nu_ref[0] - 1

    @pl.when((j <= last) & ((j == 0) | (be_ref[j] != be_ref[jnp.maximum(j - 1, 0)])))
    def _():
        WR = 128

        def cast(i, carry):
            r = pl.ds(pl.multiple_of(i * WR, WR), WR)
            wu_b[r, :] = wu_ref[r, :].astype(BF16)
            wd_b[r, :] = wd_ref[r, :].astype(BF16)
            return carry

        lax.fori_loop(0, D_MODEL // WR, cast, 0)

    def row_copy(src_row, s, r):
        return pltpu.make_async_copy(h_ref.at[pl.ds(src_row, 1), :], xbufs[s].at[pl.ds(r, 1), :], sem.at[s])

    def drain(s):
        def body(r, carry):
            row_copy(0, s, r).wait()
            return carry

        lax.fori_loop(0, blk, body, 0, unroll=8)

    @pl.when(j == 0)
    def _():
        for a in range(LOOKAHEAD):
            base = jnp.minimum(a, last) * blk

            def body(r, carry, base=base, a=a):
                row_copy(rt_ref[base + r], a, r).start()
                return carry

            lax.fori_loop(0, blk, body, 0, unroll=8)

    def live_step(s):
        drain(s)
        base = jnp.minimum(j + LOOKAHEAD, last) * blk
        for r in range(blk):
            row_copy(rt_ref[base + r], (s + LOOKAHEAD) % N_SLOTS, r).start()
        up = _dot(xbufs[s][...].astype(BF16), wu_b[...]) + bu_ref[...]
        glu = jnp.minimum(up[:, :D_FF], SWIGLU_LIMIT)
        lin = jnp.clip(up[:, D_FF:], -SWIGLU_LIMIT, SWIGLU_LIMIT)
        act = glu * jax.nn.sigmoid(SWIGLU_ALPHA * glu) * (lin + 1.0)
        o_ref[...] = _dot(act.astype(BF16), wd_b[...]) + bd_ref[...]

        @pl.when(j == last)
        def _():
            for a in range(1, N_SLOTS):
                drain((s + a) % N_SLOTS)

    for s in range(N_SLOTS):
        pl.when((j <= last) & (j % N_SLOTS == s))(functools.partial(live_step, s))

    @pl.when(j > last)
    def _():
        o_ref[...] = jnp.zeros_like(o_ref)


def _experts(h, row_tok, block_expert, n_used, w_up, b_up, w_down, b_down, layer, blk):
    n_rows = row_tok.shape[0]
    assert D_FF == D_MODEL
    wmap = lambda j, be, nu, rt: (layer, be[j], 0, 0)
    return pl.pallas_call(
        functools.partial(_expert_kernel, blk=blk),
        grid_spec=pltpu.PrefetchScalarGridSpec(
            num_scalar_prefetch=3,
            grid=(n_rows // blk,),
            in_specs=[
                pl.BlockSpec(memory_space=pl.ANY),
                pl.BlockSpec((None, None, D_MODEL, 2 * D_FF), wmap),
                pl.BlockSpec((None, None, 1, 2 * D_FF), wmap),
                pl.BlockSpec((None, None, D_FF, D_MODEL), wmap),
                pl.BlockSpec((None, None, 1, D_MODEL), wmap),
            ],
            out_specs=pl.BlockSpec((blk, D_MODEL), lambda j, be, nu, rt: (j, 0)),
            scratch_shapes=[pltpu.VMEM((blk, D_MODEL), F32)] * N_SLOTS
            + [pltpu.SemaphoreType.DMA((N_SLOTS,)), pltpu.VMEM((D_MODEL, 2 * D_FF), BF16),
               pltpu.VMEM((D_FF, D_MODEL), BF16)],
        ),
        out_shape=jax.ShapeDtypeStruct((n_rows, D_MODEL), F32),
        compiler_params=_cparams(("arbitrary",), vmem_mb=56),
        name="experts",
    )(block_expert, n_used, row_tok, h, w_up, b_up, w_down, b_down)


def _final_kernel(pos_ref, h_ref, y_ref, gate_ref, p_ref, wg_ref, wp_ref, lg_ref, lb_ref, o_ref, ob_ref,
                  *scratch, alpha, tm):
    ybufs, sem = scratch[:N_SLOTS], scratch[N_SLOTS]
    j = pl.program_id(0)
    last = pl.num_programs(0) - 1

    def row_copy(src_row, s, k, r):
        return pltpu.make_async_copy(y_ref.at[pl.ds(src_row, 1), :], ybufs[s].at[k, pl.ds(r, 1), :], sem.at[s])

    def drain(s):
        for k in range(TOP_K):
            def body(r, carry, k=k):
                row_copy(0, s, k, r).wait()
                return carry

            lax.fori_loop(0, tm, body, 0, unroll=8)

    @pl.when(j == 0)
    def _():
        for a in range(LOOKAHEAD):
            base = jnp.minimum(a, last) * (tm * TOP_K)
            for k in range(TOP_K):
                def body(r, carry, base=base, a=a, k=k):
                    row_copy(pos_ref[base + r * TOP_K + k], a, k, r).start()
                    return carry

                lax.fori_loop(0, tm, body, 0, unroll=8)

    def step(s):
        drain(s)
        base = jnp.minimum(j + LOOKAHEAD, last) * (tm * TOP_K)
        for r in range(tm):
            for k in range(TOP_K):
                row_copy(pos_ref[base + r * TOP_K + k], (s + LOOKAHEAD) % N_SLOTS, k, r).start()
        ffn = gate_ref[:, 0:1] * ybufs[s][0]
        for k in range(1, TOP_K):
            ffn = ffn + gate_ref[:, k:k + 1] * ybufs[s][k]
        h = h_ref[...]
        ple = jax.nn.sigmoid(_dot(h.astype(BF16), wg_ref[...])) * _dot(p_ref[...].astype(BF16), wp_ref[...])
        out = _ln(alpha * h + ffn + ple, lg_ref[...], lb_ref[...])
        o_ref[...] = out
        ob_ref[...] = out.astype(BF16)

        @pl.when(j == last)
        def _():
            for a in range(1, N_SLOTS):
                drain((s + a) % N_SLOTS)

    for s in range(N_SLOTS):
        pl.when(j % N_SLOTS == s)(functools.partial(step, s))


def _final(h, y_rows, pos, gate, p, ple_w_gate_bf, ple_w_proj_bf, ln_g, ln_b, layer, alpha):
    T = h.shape[0]
    tm = min(T, 128)
    lay2 = lambda t, ps: (layer, 0, 0)
    tile = lambda t, ps: (t, 0)
    return pl.pallas_call(
        functools.partial(_final_kernel, alpha=alpha, tm=tm),
        grid_spec=pltpu.PrefetchScalarGridSpec(
            num_scalar_prefetch=1,
            grid=(T // tm,),
            in_specs=[
                pl.BlockSpec((tm, D_MODEL), tile),
                pl.BlockSpec(memory_space=pl.ANY),
                pl.BlockSpec((tm, TOP_K), tile),
                pl.BlockSpec((None, tm, PLE_DIM), lambda t, ps: (layer, t, 0)),
                pl.BlockSpec((None, D_MODEL, D_MODEL), lay2),
                pl.BlockSpec((None, PLE_DIM, D_MODEL), lay2),
                pl.BlockSpec((None, 1, D_MODEL), lay2),
                pl.BlockSpec((None, 1, D_MODEL), lay2),
            ],
            out_specs=[pl.BlockSpec((tm, D_MODEL), tile)] * 2,
            scratch_shapes=[pltpu.VMEM((TOP_K, tm, D_MODEL), F32)] * N_SLOTS
            + [pltpu.SemaphoreType.DMA((N_SLOTS,))],
        ),
        out_shape=[jax.ShapeDtypeStruct((T, D_MODEL), F32), jax.ShapeDtypeStruct((T, D_MODEL), BF16)],
        compiler_params=_cparams(("arbitrary",), vmem_mb=48),
        name="combine_ple_ln2",
    )(pos, h, y_rows, gate, p, ple_w_gate_bf, ple_w_proj_bf, ln_g, ln_b)


def _moe_route(top_idx, blk):
    T = top_idx.shape[0]
    n_pairs = T * TOP_K
    pos, counts = _route_positions(top_idx, blk)
    pos = pos.reshape(n_pairs)
    pad_end = jnp.cumsum((counts + blk - 1) // blk * blk)
    n_blocks = -(-n_pairs // blk) + N_EXPERTS
    n_rows = n_blocks * blk
    tok = jnp.arange(n_pairs, dtype=jnp.int32) // TOP_K
    row_tok = (jnp.arange(n_rows, dtype=jnp.int32) % T).at[pos].set(
        tok, unique_indices=True, mode='promise_in_bounds')
    block_start = jnp.arange(n_blocks, dtype=jnp.int32) * blk
    block_expert = jnp.minimum(jnp.sum((pad_end[None, :] <= block_start[:, None]).astype(jnp.int32), axis=1),
                               N_EXPERTS - 1)
    n_used = pad_end[-1:].astype(jnp.int32) // blk
    return row_tok, block_expert, n_used, pos


def _moe_ffn_rows(h, top_idx, w_up, b_up, w_down, b_down, layer, blk):
    row_tok, block_expert, n_used, pos = _moe_route(top_idx, blk)
    return _experts(h, row_tok, block_expert, n_used, w_up, b_up, w_down, b_down, layer, blk), pos


def kernel(x_prompt, x_sample, state_ret, cache_dil_k, cache_dil_v, p_prompt, p_sample, w_in, sgu_ln_g, sgu_ln_b, sgu_w, sgu_b, w_branch, w_out, ln1_g, ln1_b, router_w, router_b, exp_w_up, exp_b_up, exp_w_down, exp_b_down, ple_w_gate, ple_w_proj, ln2_g, ln2_b):
    B, S, D = x_prompt.shape
    Bd, C, _ = x_sample.shape
    depth = w_in.shape[0]
    w_buf = cache_dil_k.shape[2]
    alpha = (2 * depth) ** 0.25
    Tp, Ts = B * S, Bd * C
    H, hd = DIL_HEADS, DIL_HD

    w_in_bf, w_branch_bf, w_out_bf = w_in.astype(BF16), w_branch.astype(BF16), w_out.astype(BF16)
    ple_w_gate_bf, ple_w_proj_bf = ple_w_gate.astype(BF16), ple_w_proj.astype(BF16)
    row3 = lambda a: a.reshape(depth, 1, a.shape[-1])
    sgu_ln_g3, sgu_ln_b3 = row3(sgu_ln_g), row3(sgu_ln_b)
    ln1_g3, ln1_b3, ln2_g3, ln2_b3 = row3(ln1_g), row3(ln1_b), row3(ln2_g), row3(ln2_b)
    router_b3 = row3(router_b)
    b_up4 = exp_b_up.reshape(depth, N_EXPERTS, 1, 2 * D_FF)
    b_down4 = exp_b_down.reshape(depth, N_EXPERTS, 1, D_MODEL)
    sgu_b4 = sgu_b.reshape(depth, SGU_GROUPS, SGU_CHUNK, 1)
    p_prompt3 = p_prompt.reshape(depth, Tp, PLE_DIM)
    p_sample3 = p_sample.reshape(depth, Ts, PLE_DIM)

    pos_p = jnp.arange(S, dtype=jnp.int32)
    pos_s = PAST_LEN + jnp.arange(C, dtype=jnp.int32)
    ret_cos_p, ret_sin_p = _rot_tables(pos_p, _retention_inv_freq())
    ret_cos_s, ret_sin_s = _rot_tables(pos_s, _retention_inv_freq())
    dil_cos_p, dil_sin_p = _rot_tables(pos_p, _rope_inv_freq(hd))
    dil_cos_s, dil_sin_s = _rot_tables(pos_s, _rope_inv_freq(hd))
    ret_tabs_p = _retention_tables(RET_CHUNK)
    ret_tabs_s = _retention_tables(C)
    cnt_a, cnt_b, cnt_n, tail, period = _sample_key_tables(C, w_buf)
    key_tables = (jnp.asarray(cnt_a), jnp.asarray(cnt_b), jnp.asarray(cnt_n), tail, period)

    xp, xs = x_prompt.reshape(Tp, D), x_sample.reshape(Ts, D)
    xp_bf, xs_bf = xp.astype(BF16), xs.astype(BF16)
    ret_p, ret_s, k_p, v_p, k_s, v_s, sgu_s = [], [], [], [], [], [], []

    def merge(x, out_a, out_b, out_c, cols, i):
        return _merge(out_a, out_b, out_c, cols, x, w_branch_bf, w_out_bf, ln1_g3, ln1_b3, router_w, router_b3,
                      i, alpha)

    def final(h, y_rows, pos, gate, p3, i):
        return _final(h, y_rows, pos, gate, p3, ple_w_gate_bf, ple_w_proj_bf, ln2_g3, ln2_b3, i, alpha)

    heads_last = lambda a: a.reshape(Bd, H, C, hd).transpose(0, 2, 1, 3)

    for i in range(depth):
        cols_p = _in_proj(xp_bf, w_in_bf, i)
        out_a, st = _retention_prompt(cols_p, B, S, ret_cos_p, ret_sin_p, ret_tabs_p)
        out_b = _sgu_prompt(cols_p, Tp, sgu_ln_g3, sgu_ln_b3, sgu_w, sgu_b4, i)
        out_c, kr, vr = _dilated_prompt(cols_p, B, S, dil_cos_p, dil_sin_p)
        h_p, idx_p, gate_p = merge(xp, out_a, out_b, out_c, cols_p, i)
        ret_p.append(st)
        k_p.append(kr.reshape(B, S, H, hd))
        v_p.append(vr.reshape(B, S, H, hd))
        cols_s = _in_proj(xs_bf, w_in_bf, i)
        cols3 = cols_s.reshape(Bd, C, N_IN_COLS)
        out_a, out_b, st, vn = _ret_sgu_sample(cols3, state_ret, ret_cos_s, ret_sin_s, ret_tabs_s,
                                               sgu_ln_g3, sgu_ln_b3, sgu_w, sgu_b4, i)
        out_c, kr, vr = _dilated_sample(cols3, cache_dil_k, cache_dil_v, dil_cos_s, dil_sin_s, key_tables, i)
        h_s, idx_s, gate_s = merge(xs, out_a.reshape(Ts, D), out_b.reshape(Ts, D), heads_last(out_c).reshape(Ts, D),
                                   cols_s, i)
        y_rows, pos = _moe_ffn_rows(jnp.concatenate([h_p, h_s]), jnp.concatenate([idx_p, idx_s]),
                                    exp_w_up, b_up4, exp_w_down, b_down4, i, MOE_ROWS)
        xp, xp_bf = final(h_p, y_rows, pos[:Tp * TOP_K], gate_p, p_prompt3, i)
        xs, xs_bf = final(h_s, y_rows, pos[Tp * TOP_K:], gate_s, p_sample3, i)
        ret_s.append(st)
        k_s.append(heads_last(kr))
        v_s.append(heads_last(vr))
        sgu_s.append(vn)

    return (xp.reshape(B, S, D), xs.reshape(Bd, C, D), jnp.stack(ret_p), jnp.stack(ret_s),
            jnp.stack(k_p), jnp.stack(v_p),
            jnp.stack(k_s), jnp.stack(v_s), jnp.stack(sgu_s))
```
